```python
import math
import jax, jax.numpy as jnp
from jax import lax
import numpy as np

D_MODEL = 4096
BATCH = 8
SEQ = 2048
DEPTH = 4

N_EVEN = (DEPTH + 1) // 2
N_ODD = DEPTH // 2

A_CH = 2048
A_CONV_WIDTH = 31
B_HEADS = 16
B_QK_DIM = 128
B_V_DIM = 128
B_KV_RANK = 512
IDX_HEADS = 16
IDX_DIM = 64
IDX_TOPK_MAX = 256
Q_BLOCK = 128
C_CH = 2048
C_CONV_WIDTH = 3
D_INNER = 2048
D_HEADS = 32
D_HEAD_DIM = 64
D_STATE = 128
D_GROUPS = 8
D_CONV_WIDTH = 4
D_CHUNK = 128
D_XBC = D_INNER + 2 * D_GROUPS * D_STATE
N_EXPERTS = 32
N_EXPERT_GROUPS = 8
EXPERTS_PER_GROUP = N_EXPERTS // N_EXPERT_GROUPS
TOP_K_EXPERTS = 2
D_FF_EXPERT = 384
ALPHA = (2 * DEPTH) ** 0.25
BETA = (8 * DEPTH) ** -0.25
EPS = 1e-5

EVEN_IN_SIZES = [2 * A_CH, B_HEADS * B_QK_DIM, B_KV_RANK, IDX_HEADS * IDX_DIM, IDX_DIM, IDX_HEADS]
EVEN_IN = sum(EVEN_IN_SIZES)
EVEN_OUT = A_CH + B_HEADS * B_V_DIM
ODD_IN_SIZES = [C_CH, C_CH, C_CH, D_INNER, D_XBC, D_HEADS]
ODD_IN = sum(ODD_IN_SIZES)
ODD_OUT = C_CH + D_INNER

kernel_name = "hybrid_conv_dsa_shortconv_ssd_grouped_moe_deepnorm"


def split_cols(u, sizes):
    offs = np.cumsum(sizes)[:-1].tolist()
    return jnp.split(u, offs, axis=-1)


def layer_norm(x, g, b):
    xf = x.astype(jnp.float32)
    mu = jnp.mean(xf, -1, keepdims=True)
    var = jnp.mean(jnp.square(xf - mu), -1, keepdims=True)
    y = (xf - mu) * lax.rsqrt(var + EPS)
    return (y * g.astype(jnp.float32) + b.astype(jnp.float32)).astype(x.dtype)


def rms_norm(x, g):
    xf = x.astype(jnp.float32)
    y = xf * lax.rsqrt(jnp.mean(xf * xf, -1, keepdims=True) + EPS)
    return (y * g.astype(jnp.float32)).astype(x.dtype)


def causal_depthwise_conv(x, w, b=None):
    k, ch = w.shape
    xp = jnp.pad(x, ((0, 0), (k - 1, 0), (0, 0)))
    y = lax.conv_general_dilated(xp, w[:, None, :].astype(x.dtype), window_strides=(1,), padding='VALID',
                                 dimension_numbers=('NWC', 'WIO', 'NWC'), feature_group_count=ch)
    if b is not None:
        y = y + b.astype(y.dtype)
    return y


def conformer_conv(u, conv_w, conv_b, ln_g, ln_b):
    a, g = jnp.split(u, 2, axis=-1)
    h = a * jax.nn.sigmoid(g)
    h = causal_depthwise_conv(h, conv_w, conv_b)
    return jax.nn.silu(layer_norm(h, ln_g, ln_b))


def dsa_attention(q, c_kv, q_idx, k_idx, w_idx, kv_norm_g, w_uk, w_uv):
    bsz, L = q.shape[0], q.shape[1]
    topk = min(IDX_TOPK_MAX, L // 4)
    nblk = L // Q_BLOCK
    c_kv = rms_norm(c_kv, kv_norm_g)
    q_lat = jnp.einsum('blhd,hdr->blhr', q, w_uk) * (B_QK_DIM ** -0.5)
    w_idx = w_idx * ((IDX_HEADS * IDX_DIM) ** -0.5)
    key_pos = jnp.arange(L)

    def to_blocks(t):
        return jnp.moveaxis(t.reshape((bsz, nblk, Q_BLOCK) + t.shape[2:]), 1, 0)

    def block(args):
        qlat_b, qidx_b, widx_b, start = args
        qpos = start + jnp.arange(Q_BLOCK)
        causal = key_pos[None, :] <= qpos[:, None]
        logits = jnp.einsum('bqhd,bsd->bqhs', qidx_b, k_idx)
        score = jnp.einsum('bqhs,bqh->bqs', jax.nn.relu(logits), widx_b).astype(jnp.float32)
        score = jnp.where(causal[None], score, -jnp.inf)
        _, idx = lax.top_k(score, topk)
        valid = idx <= qpos[None, :, None]
        kv_sel = jax.vmap(lambda c, i: c[i])(c_kv, idx)
        s = jnp.einsum('bqhr,bqkr->bqhk', qlat_b, kv_sel).astype(jnp.float32)
        s = jnp.where(valid[:, :, None, :], s, -jnp.inf)
        p = jax.nn.softmax(s, axis=-1).astype(kv_sel.dtype)
        o_lat = jnp.einsum('bqhk,bqkr->bqhr', p, kv_sel)
        return jnp.einsum('bqhr,hrd->bqhd', o_lat, w_uv)

    starts = jnp.arange(nblk) * Q_BLOCK
    out = lax.map(block, (to_blocks(q_lat), to_blocks(q_idx), to_blocks(w_idx), starts))
    return jnp.moveaxis(out, 0, 1).reshape(bsz, L, B_HEADS * B_V_DIM)


def short_gated_conv(bg, cg, h, conv_w):
    return bg * causal_depthwise_conv(cg * h, conv_w)


def ssd_chunked(x, a, b, c):
    bsz, L, H, P = x.shape
    G, N = b.shape[2], b.shape[3]
    R = H // G
    nc = L // D_CHUNK
    x = x.reshape(bsz, nc, D_CHUNK, G, R, P)
    a = a.reshape(bsz, nc, D_CHUNK, G, R)
    b = b.reshape(bsz, nc, D_CHUNK, G, N)
    c = c.reshape(bsz, nc, D_CHUNK, G, N)
    a_cum = jnp.cumsum(a, axis=2)
    seg = a_cum[:, :, :, None] - a_cum[:, :, None, :]
    tri = jnp.tril(jnp.ones((D_CHUNK, D_CHUNK), dtype=bool))[None, None, :, :, None, None]
    decay = jnp.exp(jnp.where(tri, seg, -jnp.inf))
    cb = jnp.einsum('bclgn,bcsgn->bclsg', c, b)
    y_diag = jnp.einsum('bclsgr,bcsgrp->bclgrp', cb[..., None] * decay, x)
    decay_to_end = jnp.exp(a_cum[:, :, -1:] - a_cum)
    states = jnp.einsum('bcsgn,bcsgrp->bcgrpn', b, x * decay_to_end[..., None])
    chunk_decay = jnp.exp(a_cum[:, :, -1])

    def step(h, inp):
        s_c, d_c = inp
        return h * d_c[..., None, None] + s_c, h

    h0 = jnp.zeros((bsz, G, R, P, N), x.dtype)
    _, h_prev = lax.scan(step, h0, (jnp.moveaxis(states, 1, 0), jnp.moveaxis(chunk_decay, 1, 0)))
    h_prev = jnp.moveaxis(h_prev, 0, 1)
    y_off = jnp.einsum('bclgn,bcgrpn->bclgrp', c, h_prev) * jnp.exp(a_cum)[..., None]
    return (y_diag + y_off).reshape(bsz, L, H, P)


def mamba2_mixer(z, xbc, dt, conv_w, conv_b, a_log, dt_bias, d_skip, norm_g):
    bsz, L = z.shape[0], z.shape[1]
    f32 = jnp.float32
    xbc = jax.nn.silu(causal_depthwise_conv(xbc, conv_w, conv_b))
    xs, bm, cm = split_cols(xbc, [D_INNER, D_GROUPS * D_STATE, D_GROUPS * D_STATE])
    xs = xs.reshape(bsz, L, D_HEADS, D_HEAD_DIM).astype(f32)
    bm = bm.reshape(bsz, L, D_GROUPS, D_STATE).astype(f32)
    cm = cm.reshape(bsz, L, D_GROUPS, D_STATE).astype(f32)
    dt = jax.nn.softplus(dt.astype(f32) + dt_bias.astype(f32))
    A = -jnp.exp(a_log.astype(f32))
    y = ssd_chunked(xs * dt[..., None], dt * A, bm, cm)
    y = y + xs * d_skip.astype(f32)[:, None]
    y = y.reshape(bsz, L, D_INNER) * jax.nn.silu(z.astype(f32))
    return rms_norm(y, norm_g).astype(z.dtype)


def grouped_moe(x, router_w, router_bias, w_gate, w_up, w_down):
    bsz, L, d = x.shape
    f32 = jnp.float32
    t = x.reshape(-1, d)
    scores = jax.nn.sigmoid(jnp.dot(t, router_w).astype(f32))
    sel = (scores + router_bias.astype(f32)).reshape(-1, N_EXPERT_GROUPS, EXPERTS_PER_GROUP)
    group_score = jnp.sum(lax.top_k(sel, TOP_K_EXPERTS)[0], axis=-1)
    g_best = jnp.argmax(group_score, axis=-1)
    in_group = jnp.take_along_axis(sel, g_best[:, None, None], axis=1)[:, 0]
    _, local = lax.top_k(in_group, TOP_K_EXPERTS)
    expert_ids = g_best[:, None] * EXPERTS_PER_GROUP + local
    w = jnp.take_along_axis(scores, expert_ids, axis=1)
    w = w / jnp.sum(w, axis=-1, keepdims=True)
    gates = jnp.sum(jax.nn.one_hot(expert_ids, N_EXPERTS, dtype=f32) * w[..., None], axis=1)
    h = jax.nn.silu(jnp.einsum('td,edf->tef', t, w_gate)) * jnp.einsum('td,edf->tef', t, w_up)
    h = h * gates[:, :, None].astype(h.dtype)
    return jnp.einsum('tef,efd->td', h, w_down).reshape(bsz, L, d)


def setup_inputs(seed: int = 0) -> dict:
    key = jax.random.key(seed)
    ks = iter(jax.random.split(key, 40))
    f = jnp.float32

    def nrm(shape, scale):
        return jax.random.normal(next(ks), shape, f) * scale

    x = nrm((BATCH, SEQ, D_MODEL), 1.0)
    even_w_in = nrm((N_EVEN, D_MODEL, EVEN_IN), D_MODEL ** -0.5)
    even_conv_a_w = nrm((N_EVEN, A_CONV_WIDTH, A_CH), A_CONV_WIDTH ** -0.5)
    even_conv_a_b = nrm((N_EVEN, A_CH), 0.02)
    even_ln_a_g = 1.0 + nrm((N_EVEN, A_CH), 0.02)
    even_ln_a_b = nrm((N_EVEN, A_CH), 0.02)
    even_kv_norm_g = 1.0 + nrm((N_EVEN, B_KV_RANK), 0.02)
    even_w_uk = nrm((N_EVEN, B_HEADS, B_QK_DIM, B_KV_RANK), B_KV_RANK ** -0.5)
    even_w_uv = nrm((N_EVEN, B_HEADS, B_KV_RANK, B_V_DIM), B_KV_RANK ** -0.5)
    even_w_out = nrm((N_EVEN, EVEN_OUT, D_MODEL), BETA * EVEN_OUT ** -0.5)
    odd_w_in = nrm((N_ODD, D_MODEL, ODD_IN), D_MODEL ** -0.5)
    odd_conv_c_w = nrm((N_ODD, C_CONV_WIDTH, C_CH), C_CONV_WIDTH ** -0.5)
    odd_conv_d_w = nrm((N_ODD, D_CONV_WIDTH, D_XBC), D_CONV_WIDTH ** -0.5)
    odd_conv_d_b = nrm((N_ODD, D_XBC), 0.02)
    odd_a_log = jnp.log(jax.random.uniform(next(ks), (N_ODD, D_HEADS), f, 1.0, 16.0))
    dt0 = jnp.exp(jax.random.uniform(next(ks), (N_ODD, D_HEADS), f, math.log(1e-3), math.log(1e-1)))
    odd_dt_bias = dt0 + jnp.log(-jnp.expm1(-dt0))
    odd_d_skip = 1.0 + nrm((N_ODD, D_HEADS), 0.1)
    odd_norm_g = 1.0 + nrm((N_ODD, D_INNER), 0.02)
    odd_w_out = nrm((N_ODD, ODD_OUT, D_MODEL), BETA * ODD_OUT ** -0.5)
    ln1_g = 1.0 + nrm((DEPTH, D_MODEL), 0.02)
    ln1_b = nrm((DEPTH, D_MODEL), 0.02)
    ln2_g = 1.0 + nrm((DEPTH, D_MODEL), 0.02)
    ln2_b = nrm((DEPTH, D_MODEL), 0.02)
    router_w = nrm((D_MODEL, N_EXPERTS), D_MODEL ** -0.5)
    router_bias = nrm((N_EXPERTS,), 0.01)
    moe_w_gate = nrm((DEPTH, N_EXPERTS, D_MODEL, D_FF_EXPERT), D_MODEL ** -0.5)
    moe_w_up = nrm((DEPTH, N_EXPERTS, D_MODEL, D_FF_EXPERT), D_MODEL ** -0.5)
    moe_w_down = nrm((DEPTH, N_EXPERTS, D_FF_EXPERT, D_MODEL), BETA * D_FF_EXPERT ** -0.5)
    return {"x": x, "even_w_in": even_w_in, "even_conv_a_w": even_conv_a_w, "even_conv_a_b": even_conv_a_b,
            "even_ln_a_g": even_ln_a_g, "even_ln_a_b": even_ln_a_b, "even_kv_norm_g": even_kv_norm_g,
            "even_w_uk": even_w_uk, "even_w_uv": even_w_uv, "even_w_out": even_w_out,
            "odd_w_in": odd_w_in, "odd_conv_c_w": odd_conv_c_w, "odd_conv_d_w": odd_conv_d_w,
            "odd_conv_d_b": odd_conv_d_b, "odd_a_log": odd_a_log, "odd_dt_bias": odd_dt_bias,
            "odd_d_skip": odd_d_skip, "odd_norm_g": odd_norm_g, "odd_w_out": odd_w_out,
            "ln1_g": ln1_g, "ln1_b": ln1_b, "ln2_g": ln2_g, "ln2_b": ln2_b,
            "router_w": router_w, "router_bias": router_bias,
            "moe_w_gate": moe_w_gate, "moe_w_up": moe_w_up, "moe_w_down": moe_w_down}


def reference(x, even_w_in, even_conv_a_w, even_conv_a_b, even_ln_a_g, even_ln_a_b, even_kv_norm_g,
              even_w_uk, even_w_uv, even_w_out, odd_w_in, odd_conv_c_w, odd_conv_d_w, odd_conv_d_b,
              odd_a_log, odd_dt_bias, odd_d_skip, odd_norm_g, odd_w_out, ln1_g, ln1_b, ln2_g, ln2_b,
              router_w, router_bias, moe_w_gate, moe_w_up, moe_w_down):
    bsz, L, _ = x.shape
    for layer in range(DEPTH):
        i = layer // 2
        if layer % 2 == 0:
            u = jnp.einsum('bld,de->ble', x, even_w_in[i])
            u_a, q, c_kv, q_idx, k_idx, w_idx = split_cols(u, EVEN_IN_SIZES)
            y_a = conformer_conv(u_a, even_conv_a_w[i], even_conv_a_b[i], even_ln_a_g[i], even_ln_a_b[i])
            y_b = dsa_attention(q.reshape(bsz, L, B_HEADS, B_QK_DIM), c_kv,
                                q_idx.reshape(bsz, L, IDX_HEADS, IDX_DIM), k_idx, w_idx,
                                even_kv_norm_g[i], even_w_uk[i], even_w_uv[i])
            mix = jnp.einsum('ble,ed->bld', jnp.concatenate([y_a, y_b], axis=-1), even_w_out[i])
        else:
            u = jnp.einsum('bld,de->ble', x, odd_w_in[i])
            bg, cg, hc, z, xbc, dt = split_cols(u, ODD_IN_SIZES)
            y_c = short_gated_conv(bg, cg, hc, odd_conv_c_w[i])
            y_d = mamba2_mixer(z, xbc, dt, odd_conv_d_w[i], odd_conv_d_b[i], odd_a_log[i],
                               odd_dt_bias[i], odd_d_skip[i], odd_norm_g[i])
            mix = jnp.einsum('ble,ed->bld', jnp.concatenate([y_c, y_d], axis=-1), odd_w_out[i])
        x = layer_norm(ALPHA * x + mix, ln1_g[layer], ln1_b[layer])
        ffn = grouped_moe(x, router_w, router_bias, moe_w_gate[layer], moe_w_up[layer], moe_w_down[layer])
        x = layer_norm(ALPHA * x + ffn, ln2_g[layer], ln2_b[layer])
    return x
```

```python
import functools
import math

import jax
import jax.numpy as jnp
from jax import lax
from jax.experimental import pallas as pl
from jax.experimental.pallas import tpu as pltpu

F32 = jnp.float32
BF16 = jnp.bfloat16

VMEM_LIMIT_BYTES = 56 * 1024 * 1024
LANES = 128
HALO = 32
NEG_BIG = -1e30

EPS = 1e-5


def _cparams(*sem):
    return pltpu.CompilerParams(dimension_semantics=sem, vmem_limit_bytes=VMEM_LIMIT_BYTES)


def _sigmoid(v):
    return 1.0 / (1.0 + jnp.exp(-v))


def _silu(v):
    return v * _sigmoid(v)


def _mm_kernel(x_ref, w_ref, o_ref):
    o_ref[...] = jnp.dot(x_ref[...], w_ref[...], preferred_element_type=F32).astype(o_ref.dtype)


def matmul(x, w, out_dtype, tm, tn):
    m, k = x.shape
    n = w.shape[1]
    return pl.pallas_call(
        _mm_kernel,
        grid=(n // tn, m // tm),
        in_specs=[pl.BlockSpec((tm, k), lambda j, i: (i, 0)),
                  pl.BlockSpec((k, tn), lambda j, i: (0, j))],
        out_specs=pl.BlockSpec((tm, tn), lambda j, i: (i, j)),
        out_shape=jax.ShapeDtypeStruct((m, n), out_dtype),
        compiler_params=_cparams("parallel", "parallel"),
        name="proj_matmul",
    )(x, w)


def _mm_small_kernel(x_ref, w_ref, wt_ref, o_ref, ot_ref):
    xb = x_ref[...]
    o_ref[...] = jnp.dot(xb, w_ref[...], preferred_element_type=F32)
    ot_ref[...] = lax.dot_general(wt_ref[...], xb, (((1,), (1,)), ((), ())), preferred_element_type=F32)


def matmul_small(x, w, wt, tm):
    m, k = x.shape
    n = w.shape[1]
    return pl.pallas_call(
        _mm_small_kernel,
        grid=(m // tm,),
        in_specs=[pl.BlockSpec((tm, k), lambda i: (i, 0)),
                  pl.BlockSpec((k, n), lambda i: (0, 0)),
                  pl.BlockSpec((n, k), lambda i: (0, 0))],
        out_specs=[pl.BlockSpec((tm, n), lambda i: (i, 0)),
                   pl.BlockSpec((n, tm), lambda i: (0, i))],
        out_shape=[jax.ShapeDtypeStruct((m, n), F32), jax.ShapeDtypeStruct((n, m), F32)],
        compiler_params=_cparams("parallel"),
        name="proj_small",
    )(x, w, wt)


def _layer_norm_rows(v, g, b):
    mu = jnp.mean(v, axis=-1, keepdims=True)
    d = v - mu
    var = jnp.mean(d * d, axis=-1, keepdims=True)
    return d * lax.rsqrt(var + EPS) * g + b


def _mm_ln_kernel(y1_ref, y2_ref, w1_ref, w2_ref, x_ref, g_ref, b_ref, o_ref, ob_ref, *, nj, tn, alpha):
    j = pl.program_id(1)
    acc = jnp.dot(y1_ref[...], w1_ref[...], preferred_element_type=F32)
    acc = acc + jnp.dot(y2_ref[...], w2_ref[...], preferred_element_type=F32)
    pre = acc + alpha * x_ref[...]
    for jj in range(nj):
        @pl.when(j == jj)
        def _(jj=jj):
            o_ref[:, jj * tn:(jj + 1) * tn] = pre

    @pl.when(j == nj - 1)
    def _():
        y = _layer_norm_rows(o_ref[...], g_ref[...], b_ref[...])
        o_ref[...] = y
        ob_ref[...] = y.astype(BF16)


def matmul_residual_ln(y1, y2, w1, w2, x, g, b, alpha, tm, tn):
    m, k1 = y1.shape
    k2 = y2.shape[1]
    n = w1.shape[1]
    nj = n // tn
    kern = functools.partial(_mm_ln_kernel, nj=nj, tn=tn, alpha=alpha)
    return pl.pallas_call(
        kern,
        grid=(m // tm, nj),
        in_specs=[pl.BlockSpec((tm, k1), lambda i, j: (i, 0)),
                  pl.BlockSpec((tm, k2), lambda i, j: (i, 0)),
                  pl.BlockSpec((k1, tn), lambda i, j: (0, j)),
                  pl.BlockSpec((k2, tn), lambda i, j: (0, j)),
                  pl.BlockSpec((tm, tn), lambda i, j: (i, j)),
                  pl.BlockSpec((1, n), lambda i, j: (0, 0)),
                  pl.BlockSpec((1, n), lambda i, j: (0, 0))],
        out_specs=[pl.BlockSpec((tm, n), lambda i, j: (i, 0)),
                   pl.BlockSpec((tm, n), lambda i, j: (i, 0))],
        out_shape=[jax.ShapeDtypeStruct((m, n), F32), jax.ShapeDtypeStruct((m, n), BF16)],
        compiler_params=_cparams("parallel", "arbitrary"),
        name="outproj_ln",
    )(y1, y2, w1, w2, x, g, b)


CONV_ROWS = 32
CONV_LANES = 512


SUBLANES = 8


def _conv_shifts(taps):
    off = HALO - (taps - 1)
    return sorted({(off + k) % SUBLANES for k in range(taps)} | {0})


def _dwconv_rows(hs, w_ref, b_ref, dst, *, tl, taps, post):
    c = dst.shape[1]
    off = HALO - (taps - 1)
    shifts = _conv_shifts(taps)
    nr = tl + HALO - SUBLANES
    for si, s in enumerate(shifts):
        if s:
            hs[si, 0:nr, :] = hs[0, s:s + nr, :]

    def row_body(r, carry):
        r0 = pl.multiple_of(r * CONV_ROWS, CONV_ROWS)
        for cj in range(c // CONV_LANES):
            cs = slice(cj * CONV_LANES, (cj + 1) * CONV_LANES)
            if b_ref is None:
                acc = jnp.zeros((CONV_ROWS, CONV_LANES), F32)
            else:
                acc = jnp.broadcast_to(b_ref[:, cs], (CONV_ROWS, CONV_LANES))
            for k in range(taps):
                o = off + k
                si = shifts.index(o % SUBLANES)
                acc = acc + hs[si, pl.ds(r0 + (o - o % SUBLANES), CONV_ROWS), cs] * w_ref[k:k + 1, cs]
            dst[pl.ds(r0, CONV_ROWS), cs] = post(acc).astype(dst.dtype)
        return carry

    lax.fori_loop(0, tl // CONV_ROWS, row_body, 0)


def _conformer_kernel(a_ref, g_ref, ap_ref, gp_ref, ckv_ref, w_ref, cb_ref, lg_ref, lb_ref, kg_ref,
                      o_ref, ckvn_ref, hbuf, cbuf, *, tl, taps, tiles_per_seq):
    i = pl.program_id(0)
    hp = ap_ref[...].astype(F32) * _sigmoid(gp_ref[...].astype(F32))
    hbuf[0, 0:HALO, :] = jnp.where(i % tiles_per_seq != 0, hp, 0.0)
    hbuf[0, HALO:HALO + tl, :] = a_ref[...].astype(F32) * _sigmoid(g_ref[...].astype(F32))
    _dwconv_rows(hbuf, w_ref, cb_ref, cbuf, tl=tl, taps=taps, post=lambda v: v)

    def ln_body(r, carry):
        r0 = pl.multiple_of(r * CONV_ROWS, CONV_ROWS)
        y = _layer_norm_rows(cbuf[pl.ds(r0, CONV_ROWS), :], lg_ref[...], lb_ref[...])
        o_ref[pl.ds(r0, CONV_ROWS), :] = _silu(y).astype(o_ref.dtype)
        return carry

    lax.fori_loop(0, tl // CONV_ROWS, ln_body, 0)

    ck = ckv_ref[...].astype(F32)
    ckvn_ref[...] = (ck * lax.rsqrt(jnp.mean(ck * ck, axis=-1, keepdims=True) + EPS) * kg_ref[...]).astype(ckvn_ref.dtype)


def conformer_conv(u, col_a, col_g, col_ckv, ch, rank, conv_w, conv_b, ln_g, ln_b, kv_g, seq, tl):
    t = u.shape[0]
    taps = conv_w.shape[0]
    wpad = jnp.zeros((HALO, ch), F32).at[:taps].set(conv_w)
    hb = tl // HALO
    prev = lambda col: (lambda i: (jnp.maximum(i * hb - 1, 0), col))
    kern = functools.partial(_conformer_kernel, tl=tl, taps=taps, tiles_per_seq=seq // tl)
    vec = lambda n: pl.BlockSpec((1, n), lambda i: (0, 0))
    return pl.pallas_call(
        kern,
        grid=(t // tl,),
        in_specs=[pl.BlockSpec((tl, ch), lambda i: (i, col_a)),
                  pl.BlockSpec((tl, ch), lambda i: (i, col_g)),
                  pl.BlockSpec((HALO, ch), prev(col_a)),
                  pl.BlockSpec((HALO, ch), prev(col_g)),
                  pl.BlockSpec((tl, rank), lambda i: (i, col_ckv)),
                  pl.BlockSpec((HALO, ch), lambda i: (0, 0)),
                  vec(ch), vec(ch), vec(ch), vec(rank)],
        out_specs=[pl.BlockSpec((tl, ch), lambda i: (i, 0)),
                   pl.BlockSpec((tl, rank), lambda i: (i, 0))],
        out_shape=[jax.ShapeDtypeStruct((t, ch), BF16), jax.ShapeDtypeStruct((t, rank), BF16)],
        scratch_shapes=[pltpu.VMEM((len(_conv_shifts(taps)), HALO + tl, ch), F32), pltpu.VMEM((tl, ch), F32)],
        compiler_params=_cparams("parallel"),
        name="conformer_conv",
    )(u, u, u, u, u, wpad, conv_b.reshape(1, ch), ln_g.reshape(1, ch), ln_b.reshape(1, ch), kv_g.reshape(1, rank))


def _short_conv_kernel(bg_ref, cg_ref, h_ref, cgp_ref, hp_ref, w_ref, o_ref, hbuf, cbuf, *, tl, taps, tiles_per_seq):
    i = pl.program_id(0)
    hp = cgp_ref[...].astype(F32) * hp_ref[...].astype(F32)
    hbuf[0, 0:HALO, :] = jnp.where(i % tiles_per_seq != 0, hp, 0.0)
    hbuf[0, HALO:HALO + tl, :] = cg_ref[...].astype(F32) * h_ref[...].astype(F32)
    _dwconv_rows(hbuf, w_ref, None, cbuf, tl=tl, taps=taps, post=lambda v: v)
    o_ref[...] = (bg_ref[...].astype(F32) * cbuf[...]).astype(o_ref.dtype)


def short_gated_conv(u, col_bg, col_cg, col_h, ch, conv_w, seq, tl):
    t = u.shape[0]
    taps = conv_w.shape[0]
    wpad = jnp.zeros((8, ch), F32).at[:taps].set(conv_w)
    hb = tl // HALO
    prev = lambda col: (lambda i: (jnp.maximum(i * hb - 1, 0), col))
    kern = functools.partial(_short_conv_kernel, tl=tl, taps=taps, tiles_per_seq=seq // tl)
    return pl.pallas_call(
        kern,
        grid=(t // tl,),
        in_specs=[pl.BlockSpec((tl, ch), lambda i: (i, col_bg)),
                  pl.BlockSpec((tl, ch), lambda i: (i, col_cg)),
                  pl.BlockSpec((tl, ch), lambda i: (i, col_h)),
                  pl.BlockSpec((HALO, ch), prev(col_cg)),
                  pl.BlockSpec((HALO, ch), prev(col_h)),
                  pl.BlockSpec((8, ch), lambda i: (0, 0))],
        out_specs=pl.BlockSpec((tl, ch), lambda i: (i, 0)),
        out_shape=jax.ShapeDtypeStruct((t, ch), BF16),
        scratch_shapes=[pltpu.VMEM((len(_conv_shifts(taps)), HALO + tl, ch), F32), pltpu.VMEM((tl, ch), F32)],
        compiler_params=_cparams("parallel"),
        name="short_gated_conv",
    )(u, u, u, u, u, wpad)


QBLK = 128
INT_MIN = -(2 ** 31)
M_INIT = -3.0e38


def _dsa_kernel(q_ref, qi_ref, kwq_ref, ckv_ref, kidx_ref, wuk_ref, wuv_ref, o_ref,
                keys_ref, bias_ref, qlat_ref, acc_ref, m_ref, l_ref,
                *, lk, q0, topk, heads, idx_heads, idx_dim, qk_dim, v_dim, tk, hg):
    j = pl.program_id(1)
    qbase = q0 + j * QBLK
    nkc = lk // tk
    contract_last = (((1,), (1,)), ((), ()))

    idx_scale = float(idx_heads * idx_dim) ** -0.5
    widx = kwq_ref[:, idx_dim:idx_dim + idx_heads] * idx_scale
    rowpos = qbase + lax.broadcasted_iota(jnp.int32, (QBLK, tk), 0)
    for kc in range(nkc):
        kk = kidx_ref[kc * tk:(kc + 1) * tk, 0:idx_dim].astype(BF16)
        sc = jnp.zeros((QBLK, tk), F32)
        for h in range(idx_heads):
            lg = lax.dot_general(qi_ref[:, h * idx_dim:(h + 1) * idx_dim], kk, contract_last,
                                 preferred_element_type=F32)
            sc = sc + jnp.maximum(lg, 0.0) * widx[:, h:h + 1]
        colpos = kc * tk + lax.broadcasted_iota(jnp.int32, (QBLK, tk), 1)
        bits = lax.bitcast_convert_type(sc, jnp.int32)
        key = jnp.where(bits >= 0, bits, bits ^ jnp.int32(0x7FFFFFFF))
        keys_ref[kc] = jnp.where(colpos <= rowpos, key, jnp.int32(INT_MIN))

    def bit_body(t, tau):
        cand = tau + lax.shift_left(jnp.int32(1), 31 - t)
        cnt = jnp.zeros((QBLK, 1), F32)
        for kc in range(nkc):
            cnt = cnt + jnp.sum(jnp.where(keys_ref[kc] >= cand, 1.0, 0.0), axis=-1, keepdims=True)
        return jnp.where(cnt >= float(topk), cand, tau)

    tau = lax.fori_loop(0, 32, bit_body, jnp.full((QBLK, 1), INT_MIN, jnp.int32))
    tau = jnp.maximum(tau, jnp.int32(INT_MIN + 1))
    for kc in range(nkc):
        bias_ref[kc] = jnp.where(keys_ref[kc] >= tau, 0.0, NEG_BIG)

    qk_scale = float(qk_dim) ** -0.5
    rank = ckv_ref.shape[1]
    for h in range(heads):
        ql = jnp.dot(q_ref[:, h * qk_dim:(h + 1) * qk_dim], wuk_ref[h], preferred_element_type=F32)
        qlat_ref[h * QBLK:(h + 1) * QBLK, :] = (ql * qk_scale).astype(BF16)

    m_rows = hg * QBLK
    for g in range(heads // hg):
        m_ref[...] = jnp.full((m_rows, 1), M_INIT, F32)
        l_ref[...] = jnp.zeros((m_rows, 1), F32)
        acc_ref[...] = jnp.zeros((m_rows, rank), F32)

        def kc_body(kc, carry, g=g):
            k0 = pl.multiple_of(kc * tk, tk)
            kv = ckv_ref[pl.ds(k0, tk), :]
            s = lax.dot_general(qlat_ref[g * m_rows:(g + 1) * m_rows, :], kv, contract_last,
                                preferred_element_type=F32)
            s = (s.reshape(hg, QBLK, tk) + bias_ref[kc][None]).reshape(m_rows, tk)
            m_old = m_ref[...]
            m_new = jnp.maximum(m_old, jnp.max(s, axis=-1, keepdims=True))
            alpha = jnp.exp(m_old - m_new)
            p = jnp.exp(s - m_new)
            l_ref[...] = alpha * l_ref[...] + jnp.sum(p, axis=-1, keepdims=True)
            acc_ref[...] = alpha * acc_ref[...] + jnp.dot(p.astype(BF16), kv, preferred_element_type=F32)
            m_ref[...] = m_new
            return carry

        lax.fori_loop(0, nkc, kc_body, 0)
        o = (acc_ref[...] / l_ref[...]).astype(BF16)
        for r in range(hg):
            h = g * hg + r
            o_ref[:, h * v_dim:(h + 1) * v_dim] = jnp.dot(
                o[r * QBLK:(r + 1) * QBLK, :], wuv_ref[h], preferred_element_type=F32).astype(o_ref.dtype)


def dsa_attention(u3, usm3, ckvn3, wuk, wuv, *, col_q, col_qi, idx_dim, idx_heads, topk, sb_rows, tk, hg):
    bsz, seq, _ = u3.shape
    heads, qk_dim, rank = wuk.shape
    v_dim = wuv.shape[2]
    outs = []
    for sb in range(seq // sb_rows):
        lk = (sb + 1) * sb_rows
        qb0 = sb * (sb_rows // QBLK)
        kern = functools.partial(_dsa_kernel, lk=lk, q0=sb * sb_rows, topk=topk, heads=heads, idx_heads=idx_heads,
                                 idx_dim=idx_dim, qk_dim=qk_dim, v_dim=v_dim, tk=tk, hg=hg)
        outs.append(pl.pallas_call(
            kern,
            grid=(bsz, sb_rows // QBLK),
            in_specs=[pl.BlockSpec((None, QBLK, heads * qk_dim), lambda b, j, qb0=qb0: (b, qb0 + j, col_q)),
                      pl.BlockSpec((None, QBLK, idx_heads * idx_dim), lambda b, j, qb0=qb0: (b, qb0 + j, col_qi)),
                      pl.BlockSpec((None, QBLK, LANES), lambda b, j, qb0=qb0: (b, qb0 + j, 0)),
                      pl.BlockSpec((None, lk, rank), lambda b, j: (b, 0, 0)),
                      pl.BlockSpec((None, lk, LANES), lambda b, j: (b, 0, 0)),
                      pl.BlockSpec((heads, qk_dim, rank), lambda b, j: (0, 0, 0)),
                      pl.BlockSpec((heads, rank, v_dim), lambda b, j: (0, 0, 0))],
            out_specs=pl.BlockSpec((None, QBLK, heads * v_dim), lambda b, j: (b, j, 0)),
            out_shape=jax.ShapeDtypeStruct((bsz, sb_rows, heads * v_dim), BF16),
            scratch_shapes=[pltpu.VMEM((lk // tk, QBLK, tk), jnp.int32),
                            pltpu.VMEM((lk // tk, QBLK, tk), F32),
                            pltpu.VMEM((heads * QBLK, rank), BF16),
                            pltpu.VMEM((hg * QBLK, rank), F32),
                            pltpu.VMEM((hg * QBLK, 1), F32),
                            pltpu.VMEM((hg * QBLK, 1), F32)],
            compiler_params=_cparams("parallel", "parallel"),
            name=f"dsa_attention_sb{sb}",
        )(u3, u3, usm3, ckvn3, usm3, wuk, wuv))
    return jnp.concatenate(outs, axis=1)


def _softplus(v):
    return jnp.maximum(v, 0.0) + jnp.log1p(jnp.exp(-jnp.abs(v)))


def _split3(v):
    hi = v.astype(BF16)
    r1 = v - hi.astype(F32)
    mid = r1.astype(BF16)
    lo = (r1 - mid.astype(F32)).astype(BF16)
    return hi, mid, lo


def _dot_exact_right(v, m01):
    return sum(jnp.dot(p, m01, preferred_element_type=F32) for p in _split3(v))


def _dot_exact_left(m01, v):
    return sum(jnp.dot(m01, p, preferred_element_type=F32) for p in _split3(v))


def _mamba_kernel(z_ref, xbc_ref, xbcp_ref, dt_ref, dtt_ref, cw_ref, cb_ref, alr_ref, dbr_ref, alc_ref, dbc_ref,
                  dsk_ref, ng_ref, ex_ref, o_ref, hs, xcb, st_ref, ybuf,
                  *, q, taps, heads, hdim, groups, nstate):
    c = pl.program_id(1)
    d_inner = heads * hdim
    gn = groups * nstate
    gw = d_inner // groups
    hpg = heads // groups

    @pl.when(c == 0)
    def _():
        st_ref[...] = jnp.zeros(st_ref.shape, F32)

    hs[0, 0:HALO, :] = jnp.where(c != 0, xbcp_ref[...].astype(F32), 0.0)
    hs[0, HALO:HALO + q, :] = xbc_ref[...].astype(F32)
    _dwconv_rows(hs, cw_ref, cb_ref, xcb, tl=q, taps=taps, post=_silu)

    dt = _softplus(dt_ref[...] + dbr_ref[...])
    a = dt * (-jnp.exp(alr_ref[...]))
    dtt = _softplus(dtt_ref[...] + dbc_ref[...])
    at = dtt * (-jnp.exp(alc_ref[...]))
    ri = lax.broadcasted_iota(jnp.int32, (q, q), 0)
    ci = lax.broadcasted_iota(jnp.int32, (q, q), 1)
    lower = ri >= ci
    a_cum = _dot_exact_left(jnp.where(lower, 1.0, 0.0).astype(BF16), a)
    a_cum_t = _dot_exact_right(at, jnp.where(ri <= ci, 1.0, 0.0).astype(BF16))

    ex = ex_ref[...]
    dt_x = _dot_exact_right(dt, ex)
    acum_x = _dot_exact_right(a_cum, ex)
    alast_x = acum_x[q - 1:q, :]
    xs = xcb[:, 0:d_inner]
    xdt = xs * dt_x
    xdt_b = xdt.astype(BF16)
    xend_b = (xdt * jnp.exp(alast_x - acum_x)).astype(BF16)
    eac_x = jnp.exp(acum_x)
    cdec_x = jnp.exp(alast_x)

    for g in range(groups):
        bg = xcb[:, d_inner + g * nstate:d_inner + (g + 1) * nstate].astype(BF16)
        cg = xcb[:, d_inner + gn + g * nstate:d_inner + gn + (g + 1) * nstate].astype(BF16)
        cbm = lax.dot_general(cg, bg, (((1,), (1,)), ((), ())), preferred_element_type=F32)
        for r in range(hpg):
            h = g * hpg + r
            seg = a_cum[:, h:h + 1] - a_cum_t[h:h + 1, :]
            dec = jnp.exp(jnp.where(lower, seg, NEG_BIG))
            mm = (cbm * dec).astype(BF16)
            ybuf[:, h * hdim:(h + 1) * hdim] = jnp.dot(mm, xdt_b[:, h * hdim:(h + 1) * hdim],
                                                       preferred_element_type=F32)
        lanes = slice(g * gw, (g + 1) * gw)
        st = st_ref[g]
        ybuf[:, lanes] = ybuf[:, lanes] + jnp.dot(cg, st.astype(BF16), preferred_element_type=F32) * eac_x[:, lanes]
        new = lax.dot_general(bg, xend_b[:, lanes], (((0,), (0,)), ((), ())), preferred_element_type=F32)
        st_ref[g] = st * cdec_x[:, lanes] + new

    y = ybuf[...] + xs * dsk_ref[...]
    y = y * _silu(z_ref[...].astype(F32))
    y = y * lax.rsqrt(jnp.mean(y * y, axis=-1, keepdims=True) + EPS) * ng_ref[...]
    o_ref[...] = y.astype(o_ref.dtype)


def mamba2_mixer(u, dts, dtst, col_z, col_xbc, conv_w, conv_b, a_log, dt_bias, d_skip, norm_g,
                 *, bsz, seq, q, heads, hdim, groups, nstate):
    t = u.shape[0]
    d_inner = heads * hdim
    xw = d_inner + 2 * groups * nstate
    taps = conv_w.shape[0]
    nc = seq // q
    hb = q // HALO
    wpad = jnp.zeros((SUBLANES, xw), F32).at[:taps].set(conv_w)
    row = lambda v: jnp.zeros((1, LANES), F32).at[0, :heads].set(v)
    col = lambda v: v.reshape(heads, 1).astype(F32)
    expand = (jnp.arange(d_inner)[None, :] // hdim == jnp.arange(LANES)[:, None]).astype(BF16)
    kern = functools.partial(_mamba_kernel, q=q, taps=taps, heads=heads, hdim=hdim, groups=groups, nstate=nstate)
    rowi = lambda b, c: b * nc + c
    full = lambda shp: pl.BlockSpec(shp, lambda b, c: (0,) * len(shp))
    return pl.pallas_call(
        kern,
        grid=(bsz, nc),
        in_specs=[pl.BlockSpec((q, d_inner), lambda b, c: (rowi(b, c), col_z)),
                  pl.BlockSpec((q, xw), lambda b, c: (rowi(b, c), col_xbc)),
                  pl.BlockSpec((HALO, xw), lambda b, c: (jnp.maximum(rowi(b, c) * hb - 1, 0), col_xbc)),
                  pl.BlockSpec((q, LANES), lambda b, c: (rowi(b, c), 0)),
                  pl.BlockSpec((heads, q), lambda b, c: (0, rowi(b, c))),
                  full((SUBLANES, xw)), full((1, xw)),
                  full((1, LANES)), full((1, LANES)), full((heads, 1)), full((heads, 1)),
                  full((1, d_inner)), full((1, d_inner)), full((LANES, d_inner))],
        out_specs=pl.BlockSpec((q, d_inner), lambda b, c: (rowi(b, c), 0)),
        out_shape=jax.ShapeDtypeStruct((t, d_inner), BF16),
        scratch_shapes=[pltpu.VMEM((len(_conv_shifts(taps)), HALO + q, xw), F32),
                        pltpu.VMEM((q, xw), F32),
                        pltpu.VMEM((groups, nstate, d_inner // groups), F32),
                        pltpu.VMEM((q, d_inner), F32)],
        compiler_params=_cparams("parallel", "arbitrary"),
        name="mamba2_ssd",
    )(u, u, u, dts, dtst, wpad, conv_b.reshape(1, xw), row(a_log), row(dt_bias), col(a_log), col(dt_bias),
      jnp.repeat(d_skip.astype(F32), hdim).reshape(1, d_inner), norm_g.reshape(1, d_inner), expand)


ROUTER_ROWS = 8


def _router_kernel(x_ref, rwt_ref, rb_ref, o_ref, *, groups, epg, top_k):
    lg = lax.dot_general(rwt_ref[...], x_ref[...], (((1,), (1,)), ((), ())), preferred_element_type=F32)
    sc = _sigmoid(lg)
    sel = sc + rb_ref[...]
    s = [sel[p * groups:(p + 1) * groups, :] for p in range(epg)]
    w = [sc[p * groups:(p + 1) * groups, :] for p in range(epg)]
    gs = None
    for p in range(epg):
        for r in range(p + 1, epg):
            pair = s[p] + s[r]
            gs = pair if gs is None else jnp.maximum(gs, pair)
    gidx = lax.broadcasted_iota(jnp.int32, gs.shape, 0)
    gmax = jnp.max(gs, axis=0, keepdims=True)
    gbest = jnp.min(jnp.where(gs == gmax, gidx, groups), axis=0, keepdims=True)
    pick = gidx == gbest
    v = [jnp.sum(jnp.where(pick, s[p], 0.0), axis=0, keepdims=True) for p in range(epg)]
    wv = [jnp.sum(jnp.where(pick, w[p], 0.0), axis=0, keepdims=True) for p in range(epg)]
    chosen = []
    for p in range(epg):
        rank = jnp.zeros_like(v[p])
        for r in range(epg):
            if r == p:
                continue
            ahead = (v[r] >= v[p]) if r < p else (v[r] > v[p])
            rank = rank + jnp.where(ahead, 1.0, 0.0)
        chosen.append(jnp.where(rank < float(top_k), wv[p], 0.0))
    tot = chosen[0]
    for p in range(1, epg):
        tot = tot + chosen[p]
    o_ref[...] = jnp.zeros(o_ref.shape, F32)
    for p in range(epg):
        o_ref[p:p + 1, :] = chosen[p] / tot
    o_ref[epg:epg + 1, :] = gbest.astype(F32)


def moe_router(xb, router_w, router_bias, *, groups, epg, top_k, tm):
    t, d = xb.shape
    e = groups * epg
    order = jnp.arange(e).reshape(groups, epg).T.reshape(-1)
    rwt = router_w.T[order].astype(BF16)
    rb = router_bias[order].reshape(e, 1).astype(F32)
    kern = functools.partial(_router_kernel, groups=groups, epg=epg, top_k=top_k)
    return pl.pallas_call(
        kern,
        grid=(t // tm,),
        in_specs=[pl.BlockSpec((tm, d), lambda i: (i, 0)),
                  pl.BlockSpec((e, d), lambda i: (0, 0)),
                  pl.BlockSpec((e, 1), lambda i: (0, 0))],
        out_specs=pl.BlockSpec((ROUTER_ROWS, tm), lambda i: (0, i)),
        out_shape=jax.ShapeDtypeStruct((ROUTER_ROWS, t), F32),
        compiler_params=_cparams("parallel"),
        name="moe_router",
    )(xb, rwt, rb)


def _moe_up_kernel(tg_ref, x_ref, gate_ref, wg_ref, wu_ref, h_ref, *, epg, dff):
    x = x_ref[...]
    g = jnp.dot(x, wg_ref[...], preferred_element_type=F32)
    u = jnp.dot(x, wu_ref[...], preferred_element_type=F32)
    h = _silu(g) * u
    gates = gate_ref[...]
    for p in range(epg):
        h_ref[:, p * dff:(p + 1) * dff] = (h[:, p * dff:(p + 1) * dff] * gates[:, p:p + 1]).astype(h_ref.dtype)


def _moe_down_kernel(tg_ref, h_ref, wd_ref, x_ref, g_ref, b_ref, o_ref, *, alpha):
    ffn = jnp.dot(h_ref[...], wd_ref[...], preferred_element_type=F32)
    o_ref[...] = _layer_norm_rows(alpha * x_ref[...] + ffn, g_ref[...], b_ref[...])


def moe_experts(xs_b, xs_f, gates_s, tile_group, wg, wu, wd, ln_g, ln_b, *, alpha, epg, tm):
    p_rows, d = xs_b.shape
    f4 = wg.shape[2]
    dff = f4 // epg
    n_tiles = p_rows // tm
    once = pl.Buffered(1)
    h = pl.pallas_call(
        functools.partial(_moe_up_kernel, epg=epg, dff=dff),
        grid_spec=pltpu.PrefetchScalarGridSpec(
            num_scalar_prefetch=1,
            grid=(n_tiles,),
            in_specs=[pl.BlockSpec((tm, d), lambda i, tg: (i, 0)),
                      pl.BlockSpec((tm, epg), lambda i, tg: (i, 0)),
                      pl.BlockSpec((None, d, f4), lambda i, tg: (tg[i], 0, 0), pipeline_mode=once),
                      pl.BlockSpec((None, d, f4), lambda i, tg: (tg[i], 0, 0), pipeline_mode=once)],
            out_specs=pl.BlockSpec((tm, f4), lambda i, tg: (i, 0))),
        out_shape=jax.ShapeDtypeStruct((p_rows, f4), BF16),
        compiler_params=_cparams("arbitrary"),
        name="moe_gate_up",
    )(tile_group, xs_b, gates_s, wg, wu)
    return pl.pallas_call(
        functools.partial(_moe_down_kernel, alpha=alpha),
        grid_spec=pltpu.PrefetchScalarGridSpec(
            num_scalar_prefetch=1,
            grid=(n_tiles,),
            in_specs=[pl.BlockSpec((tm, f4), lambda i, tg: (i, 0)),
                      pl.BlockSpec((None, f4, d), lambda i, tg: (tg[i], 0, 0), pipeline_mode=once),
                      pl.BlockSpec((tm, d), lambda i, tg: (i, 0)),
                      pl.BlockSpec((1, d), lambda i, tg: (0, 0)),
                      pl.BlockSpec((1, d), lambda i, tg: (0, 0))],
            out_specs=pl.BlockSpec((tm, d), lambda i, tg: (i, 0))),
        out_shape=jax.ShapeDtypeStruct((p_rows, d), F32),
        compiler_params=_cparams("arbitrary"),
        name="moe_down_ln",
    )(tile_group, h, wd, xs_f, ln_g, ln_b)


def moe_block(x, xb, router_w, router_bias, wg, wu, wd, ln_g, ln_b, *, alpha, groups, epg, top_k, tm):
    t, d = x.shape
    rt = moe_router(xb, router_w, router_bias, groups=groups, epg=epg, top_k=top_k, tm=512)
    gbest = rt[epg].astype(jnp.int32)
    gates = rt[:epg].T
    onehot = (gbest[:, None] == jnp.arange(groups)[None, :]).astype(jnp.int32)
    counts = jnp.sum(onehot, axis=0)
    padded = ((counts + tm - 1) // tm) * tm
    ends = jnp.cumsum(padded)
    rank = jnp.take_along_axis(jnp.cumsum(onehot, axis=0) - onehot, gbest[:, None], axis=1)[:, 0]
    pos = (ends - padded)[gbest] + rank
    p_rows = t + groups * tm
    src = jnp.zeros((p_rows,), jnp.int32).at[pos].set(jnp.arange(t, dtype=jnp.int32))
    gates_s = jnp.zeros((p_rows, epg), F32).at[pos].set(gates)
    tile_start = jnp.arange(p_rows // tm, dtype=jnp.int32) * tm
    tile_group = jnp.minimum(jnp.sum(tile_start[:, None] >= ends[None, :], axis=1), groups - 1).astype(jnp.int32)
    xs_b = xb[src]
    xs_f = x[src]
    out_s = moe_experts(xs_b, xs_f, gates_s, tile_group, wg, wu, wd, ln_g, ln_b, alpha=alpha, epg=epg, tm=tm)
    x2 = out_s[pos]
    return x2, x2.astype(BF16)


def kernel(x, even_w_in, even_conv_a_w, even_conv_a_b, even_ln_a_g, even_ln_a_b, even_kv_norm_g, even_w_uk, even_w_uv, even_w_out, odd_w_in, odd_conv_c_w, odd_conv_d_w, odd_conv_d_b, odd_a_log, odd_dt_bias, odd_d_skip, odd_norm_g, odd_w_out, ln1_g, ln1_b, ln2_g, ln2_b, router_w, router_bias, moe_w_gate, moe_w_up, moe_w_down):
    bsz, seq, d = x.shape
    t = bsz * seq
    depth = ln1_g.shape[0]
    alpha = (2 * depth) ** 0.25

    a_ch = even_conv_a_w.shape[2]
    heads, qk_dim, rank = even_w_uk.shape[1:]
    v_dim = even_w_uv.shape[3]
    idx_dim, idx_heads = 64, 16
    q_w = heads * qk_dim
    qi_w = idx_heads * idx_dim
    topk = min(256, seq // 4)

    c_ch = odd_conv_c_w.shape[2]
    d_heads = odd_a_log.shape[1]
    d_inner = odd_norm_g.shape[1]
    xw = odd_conv_d_w.shape[2]
    d_groups, d_state, d_chunk = 8, 128, 128

    n_experts, _, dff = moe_w_gate.shape[1:]
    groups, top_k = 8, 2
    epg = n_experts // groups

    xf = x.reshape(t, d)
    xb = xf.astype(BF16)

    def pad_lanes(w):
        return jnp.pad(w, ((0, 0), (0, LANES - w.shape[1])))

    for layer in range(depth):
        i = layer // 2
        if layer % 2 == 0:
            w = even_w_in[i]
            o_q = 2 * a_ch
            o_ckv = o_q + q_w
            o_qi = o_ckv + rank
            o_k = o_qi + qi_w
            w_main = jnp.concatenate([w[:, :o_ckv], w[:, o_qi:o_k], w[:, o_ckv:o_qi]], axis=1).astype(BF16)
            w_small = pad_lanes(w[:, o_k:]).astype(BF16)
            u = matmul(xb, w_main, BF16, 512, 1536)
            usm, _ = matmul_small(xb, w_small, w_small.T, 512)
            y_a, ckvn = conformer_conv(u, 0, 1, (o_q + q_w + qi_w) // rank, a_ch, rank,
                                       even_conv_a_w[i], even_conv_a_b[i], even_ln_a_g[i], even_ln_a_b[i],
                                       even_kv_norm_g[i], seq, 256)
            y_b = dsa_attention(u.reshape(bsz, seq, -1), usm.reshape(bsz, seq, LANES), ckvn.reshape(bsz, seq, rank),
                                even_w_uk[i].astype(BF16), even_w_uv[i].astype(BF16),
                                col_q=o_q // q_w, col_qi=(o_q + q_w) // qi_w, idx_dim=idx_dim, idx_heads=idx_heads,
                                topk=topk, sb_rows=512, tk=512, hg=4).reshape(t, heads * v_dim)
            w_out = even_w_out[i].astype(BF16)
            y1, y2, k1 = y_a, y_b, a_ch
        else:
            w = odd_w_in[i]
            o_dt = 3 * c_ch + d_inner + xw
            w_main = w[:, :o_dt].astype(BF16)
            w_small = pad_lanes(w[:, o_dt:]).astype(BF16)
            u = matmul(xb, w_main, BF16, 512, 1536)
            dts, dtst = matmul_small(xb, w_small, w_small.T, 512)
            y_c = short_gated_conv(u, 0, 1, 2, c_ch, odd_conv_c_w[i], seq, 256)
            y_d = mamba2_mixer(u, dts, dtst, 3 * c_ch // d_inner, (3 * c_ch + d_inner) // xw,
                               odd_conv_d_w[i], odd_conv_d_b[i], odd_a_log[i], odd_dt_bias[i], odd_d_skip[i],
                               odd_norm_g[i], bsz=bsz, seq=seq, q=d_chunk, heads=d_heads, hdim=d_inner // d_heads,
                               groups=d_groups, nstate=d_state)
            w_out = odd_w_out[i].astype(BF16)
            y1, y2, k1 = y_c, y_d, c_ch
        xf, xb = matmul_residual_ln(y1, y2, w_out[:k1], w_out[k1:], xf, ln1_g[layer].reshape(1, d),
                                    ln1_b[layer].reshape(1, d), alpha, 512, 512)
        wg = moe_w_gate[layer].reshape(groups, epg, d, dff).transpose(0, 2, 1, 3).reshape(groups, d, epg * dff)
        wu = moe_w_up[layer].reshape(groups, epg, d, dff).transpose(0, 2, 1, 3).reshape(groups, d, epg * dff)
        wd = moe_w_down[layer].reshape(groups, epg * dff, d)
        xf, xb = moe_block(xf, xb, router_w, router_bias, wg.astype(BF16), wu.astype(BF16), wd.astype(BF16),
                           ln2_g[layer].reshape(1, d), ln2_b[layer].reshape(1, d),
                           alpha=alpha, groups=groups, epg=epg, top_k=top_k, tm=256)
    return xf.reshape(bsz, seq, d)
```

```python
import functools
import math

import jax
import jax.numpy as jnp
from jax import lax
from jax.experimental import pallas as pl
from jax.experimental.pallas import tpu as pltpu

F32 = jnp.float32
BF16 = jnp.bfloat16

VMEM_LIMIT_BYTES = 56 * 1024 * 1024
LANES = 128
HALO = 32
NEG_BIG = -1e30

EPS = 1e-5


def _cparams(*sem):
    return pltpu.CompilerParams(dimension_semantics=sem, vmem_limit_bytes=VMEM_LIMIT_BYTES)


def _sigmoid(v):
    return 1.0 / (1.0 + jnp.exp(-v))


def _silu(v):
    return v * _sigmoid(v)


def _mm_kernel(x_ref, w_ref, o_ref, wb_ref):
    @pl.when(pl.program_id(1) == 0)
    def _():
        wb_ref[...] = w_ref[...].astype(BF16)

    o_ref[...] = jnp.dot(x_ref[...], wb_ref[...], preferred_element_type=F32).astype(o_ref.dtype)


def matmul(x, w, layer, n, out_dtype, tm, tn):
    m, k = x.shape
    return pl.pallas_call(
        _mm_kernel,
        grid=(n // tn, m // tm),
        in_specs=[pl.BlockSpec((tm, k), lambda j, i: (i, 0)),
                  pl.BlockSpec((None, k, tn), lambda j, i: (layer, 0, j), pipeline_mode=pl.Buffered(1))],
        out_specs=pl.BlockSpec((tm, tn), lambda j, i: (i, j)),
        out_shape=jax.ShapeDtypeStruct((m, n), out_dtype),
        scratch_shapes=[pltpu.VMEM((k, tn), BF16)],
        compiler_params=_cparams("arbitrary", "arbitrary"),
        name="proj_matmul",
    )(x, w)


def _mm_small_kernel(x_ref, w_ref, wt_ref, o_ref, ot_ref):
    xb = x_ref[...]
    o_ref[...] = jnp.dot(xb, w_ref[...], preferred_element_type=F32)
    ot_ref[...] = lax.dot_general(wt_ref[...], xb, (((1,), (1,)), ((), ())), preferred_element_type=F32)


def matmul_small(x, w, wt, tm):
    m, k = x.shape
    n = w.shape[1]
    return pl.pallas_call(
        _mm_small_kernel,
        grid=(m // tm,),
        in_specs=[pl.BlockSpec((tm, k), lambda i: (i, 0)),
                  pl.BlockSpec((k, n), lambda i: (0, 0)),
                  pl.BlockSpec((n, k), lambda i: (0, 0))],
        out_specs=[pl.BlockSpec((tm, n), lambda i: (i, 0)),
                   pl.BlockSpec((n, tm), lambda i: (0, i))],
        out_shape=[jax.ShapeDtypeStruct((m, n), F32), jax.ShapeDtypeStruct((n, m), F32)],
        compiler_params=_cparams("parallel"),
        name="proj_small",
    )(x, w, wt)


def _layer_norm_rows(v, g, b):
    mu = jnp.mean(v, axis=-1, keepdims=True)
    d = v - mu
    var = jnp.mean(d * d, axis=-1, keepdims=True)
    return d * lax.rsqrt(var + EPS) * g + b


def _mm_ln_kernel(y1_ref, y2_ref, w_ref, x_ref, g_ref, b_ref, o_ref, ob_ref, *, nj, tn, alpha):
    j = pl.program_id(1)
    y = jnp.concatenate([y1_ref[...], y2_ref[...]], axis=1)
    pre = jnp.dot(y, w_ref[...], preferred_element_type=F32) + alpha * x_ref[...]
    o_ref[:, pl.ds(pl.multiple_of(j * tn, tn), tn)] = pre

    @pl.when(j == nj - 1)
    def _():
        v = _layer_norm_rows(o_ref[...], g_ref[...], b_ref[...])
        o_ref[...] = v
        ob_ref[...] = v.astype(BF16)


def matmul_residual_ln(y1, y2, w, x, g, b, alpha, tm, tn):
    m, k1 = y1.shape
    k2 = y2.shape[1]
    n = w.shape[1]
    nj = n // tn
    kern = functools.partial(_mm_ln_kernel, nj=nj, tn=tn, alpha=alpha)
    return pl.pallas_call(
        kern,
        grid=(m // tm, nj),
        in_specs=[pl.BlockSpec((tm, k1), lambda i, j: (i, 0)),
                  pl.BlockSpec((tm, k2), lambda i, j: (i, 0)),
                  pl.BlockSpec((k1 + k2, tn), lambda i, j: (0, j)),
                  pl.BlockSpec((tm, tn), lambda i, j: (i, j)),
                  pl.BlockSpec((1, n), lambda i, j: (0, 0)),
                  pl.BlockSpec((1, n), lambda i, j: (0, 0))],
        out_specs=[pl.BlockSpec((tm, n), lambda i, j: (i, 0)),
                   pl.BlockSpec((tm, n), lambda i, j: (i, 0))],
        out_shape=[jax.ShapeDtypeStruct((m, n), F32), jax.ShapeDtypeStruct((m, n), BF16)],
        compiler_params=_cparams("parallel", "arbitrary"),
        name="outproj_ln",
    )(y1, y2, w, x, g, b)


CONV_ROWS = 32
CONV_LANES = 512


SUBLANES = 8


def _conv_shifts(taps):
    off = HALO - (taps - 1)
    return sorted({(off + k) % SUBLANES for k in range(taps)} | {0})


def _tap_weights(conv_w):
    taps, c = conv_w.shape
    return jnp.broadcast_to(conv_w.astype(F32)[:, None, :], (taps, SUBLANES, c))


def _dwconv_rows(hs, w_ref, b_ref, dst, *, tl, taps, post):
    c = dst.shape[1]
    off = HALO - (taps - 1)
    shifts = _conv_shifts(taps)
    nr = tl + HALO - SUBLANES
    for si, s in enumerate(shifts):
        if s:
            hs[si, 0:nr, :] = hs[0, s:s + nr, :]

    def row_body(r, carry):
        r0 = pl.multiple_of(r * CONV_ROWS, CONV_ROWS)
        for cj in range(c // CONV_LANES):
            cs = slice(cj * CONV_LANES, (cj + 1) * CONV_LANES)
            if b_ref is None:
                acc = jnp.zeros((CONV_ROWS, CONV_LANES), F32)
            else:
                acc = jnp.broadcast_to(b_ref[:, cs], (CONV_ROWS, CONV_LANES))
            for k in range(taps):
                o = off + k
                si = shifts.index(o % SUBLANES)
                wk = jnp.concatenate([w_ref[k, :, cs]] * (CONV_ROWS // SUBLANES), axis=0)
                acc = acc + hs[si, pl.ds(r0 + (o - o % SUBLANES), CONV_ROWS), cs] * wk
            dst[pl.ds(r0, CONV_ROWS), cs] = post(acc).astype(dst.dtype)
        return carry

    lax.fori_loop(0, tl // CONV_ROWS, row_body, 0)


def _conformer_kernel(a_ref, g_ref, ap_ref, gp_ref, ckv_ref, w_ref, cb_ref, lg_ref, lb_ref, kg_ref,
                      o_ref, ckvn_ref, hbuf, cbuf, *, tl, taps, tiles_per_seq):
    i = pl.program_id(0)
    hp = ap_ref[...].astype(F32) * _sigmoid(gp_ref[...].astype(F32))
    hbuf[0, 0:HALO, :] = jnp.where(i % tiles_per_seq != 0, hp, 0.0)
    hbuf[0, HALO:HALO + tl, :] = a_ref[...].astype(F32) * _sigmoid(g_ref[...].astype(F32))
    _dwconv_rows(hbuf, w_ref, cb_ref, cbuf, tl=tl, taps=taps, post=lambda v: v)

    def ln_body(r, carry):
        r0 = pl.multiple_of(r * CONV_ROWS, CONV_ROWS)
        y = _layer_norm_rows(cbuf[pl.ds(r0, CONV_ROWS), :], lg_ref[...], lb_ref[...])
        o_ref[pl.ds(r0, CONV_ROWS), :] = _silu(y).astype(o_ref.dtype)
        return carry

    lax.fori_loop(0, tl // CONV_ROWS, ln_body, 0)

    ck = ckv_ref[...].astype(F32)
    ckvn_ref[...] = (ck * lax.rsqrt(jnp.mean(ck * ck, axis=-1, keepdims=True) + EPS) * kg_ref[...]).astype(ckvn_ref.dtype)


def conformer_conv(u, col_a, col_g, col_ckv, ch, rank, conv_w, conv_b, ln_g, ln_b, kv_g, seq, tl):
    t = u.shape[0]
    taps = conv_w.shape[0]
    wpad = _tap_weights(conv_w)
    hb = tl // HALO
    prev = lambda col: (lambda i: (jnp.maximum(i * hb - 1, 0), col))
    kern = functools.partial(_conformer_kernel, tl=tl, taps=taps, tiles_per_seq=seq // tl)
    vec = lambda n: pl.BlockSpec((1, n), lambda i: (0, 0))
    return pl.pallas_call(
        kern,
        grid=(t // tl,),
        in_specs=[pl.BlockSpec((tl, ch), lambda i: (i, col_a)),
                  pl.BlockSpec((tl, ch), lambda i: (i, col_g)),
                  pl.BlockSpec((HALO, ch), prev(col_a)),
                  pl.BlockSpec((HALO, ch), prev(col_g)),
                  pl.BlockSpec((tl, rank), lambda i: (i, col_ckv)),
                  pl.BlockSpec((taps, SUBLANES, ch), lambda i: (0, 0, 0)),
                  vec(ch), vec(ch), vec(ch), vec(rank)],
        out_specs=[pl.BlockSpec((tl, ch), lambda i: (i, 0)),
                   pl.BlockSpec((tl, rank), lambda i: (i, 0))],
        out_shape=[jax.ShapeDtypeStruct((t, ch), BF16), jax.ShapeDtypeStruct((t, rank), BF16)],
        scratch_shapes=[pltpu.VMEM((len(_conv_shifts(taps)), HALO + tl, ch), F32), pltpu.VMEM((tl, ch), F32)],
        compiler_params=_cparams("parallel"),
        name="conformer_conv",
    )(u, u, u, u, u, wpad, conv_b.reshape(1, ch), ln_g.reshape(1, ch), ln_b.reshape(1, ch), kv_g.reshape(1, rank))


def _short_conv_kernel(bg_ref, cg_ref, h_ref, cgp_ref, hp_ref, w_ref, o_ref, hbuf, cbuf, *, tl, taps, tiles_per_seq):
    i = pl.program_id(0)
    hp = cgp_ref[...].astype(F32) * hp_ref[...].astype(F32)
    hbuf[0, 0:HALO, :] = jnp.where(i % tiles_per_seq != 0, hp, 0.0)
    hbuf[0, HALO:HALO + tl, :] = cg_ref[...].astype(F32) * h_ref[...].astype(F32)
    _dwconv_rows(hbuf, w_ref, None, cbuf, tl=tl, taps=taps, post=lambda v: v)
    o_ref[...] = (bg_ref[...].astype(F32) * cbuf[...]).astype(o_ref.dtype)


def short_gated_conv(u, col_bg, col_cg, col_h, ch, conv_w, seq, tl):
    t = u.shape[0]
    taps = conv_w.shape[0]
    wpad = _tap_weights(conv_w)
    hb = tl // HALO
    prev = lambda col: (lambda i: (jnp.maximum(i * hb - 1, 0), col))
    kern = functools.partial(_short_conv_kernel, tl=tl, taps=taps, tiles_per_seq=seq // tl)
    return pl.pallas_call(
        kern,
        grid=(t // tl,),
        in_specs=[pl.BlockSpec((tl, ch), lambda i: (i, col_bg)),
                  pl.BlockSpec((tl, ch), lambda i: (i, col_cg)),
                  pl.BlockSpec((tl, ch), lambda i: (i, col_h)),
                  pl.BlockSpec((HALO, ch), prev(col_cg)),
                  pl.BlockSpec((HALO, ch), prev(col_h)),
                  pl.BlockSpec((taps, SUBLANES, ch), lambda i: (0, 0, 0))],
        out_specs=pl.BlockSpec((tl, ch), lambda i: (i, 0)),
        out_shape=jax.ShapeDtypeStruct((t, ch), BF16),
        scratch_shapes=[pltpu.VMEM((len(_conv_shifts(taps)), HALO + tl, ch), F32), pltpu.VMEM((tl, ch), F32)],
        compiler_params=_cparams("parallel"),
        name="short_gated_conv",
    )(u, u, u, u, u, wpad)


QBLK = 128
INT_MIN = -(2 ** 31)
LOG2E = 1.4426950408889634


def _dsa_kernel(q_ref, qia_ref, qib_ref, kwq_ref, ckv_ref, kidx_ref, wuk_ref, wuv_ref, o_ref,
                keys_ref, bias_ref, qlat_ref,
                *, lk, q0, topk, heads, idx_heads, idx_dim, qk_dim, v_dim, tk, hg):
    j = pl.program_id(1)
    qbase = q0 + j * QBLK
    nkc = lk // tk
    contract_last = (((1,), (1,)), ((), ()))

    idx_scale = float(idx_heads * idx_dim) ** -0.5
    widx = kwq_ref[:, idx_dim:idx_dim + idx_heads] * idx_scale
    rowpos = qbase + lax.broadcasted_iota(jnp.int32, (QBLK, tk), 0)
    hpb = qia_ref.shape[1] // idx_dim
    for kc in range(nkc):
        kk = kidx_ref[kc * tk:(kc + 1) * tk, 0:idx_dim].astype(BF16)
        sc = jnp.zeros((QBLK, tk), F32)
        for h in range(idx_heads):
            qi_ref, hl = (qia_ref, h) if h < hpb else (qib_ref, h - hpb)
            lg = lax.dot_general(qi_ref[:, hl * idx_dim:(hl + 1) * idx_dim], kk, contract_last,
                                 preferred_element_type=F32)
            sc = sc + jnp.maximum(lg, 0.0) * widx[:, h:h + 1]
        colpos = kc * tk + lax.broadcasted_iota(jnp.int32, (QBLK, tk), 1)
        bits = lax.bitcast_convert_type(sc, jnp.int32)
        key = jnp.where(bits >= 0, bits, bits ^ jnp.int32(0x7FFFFFFF))
        keys_ref[kc] = jnp.where(colpos <= rowpos, key, jnp.int32(INT_MIN))

    def bit_body(t, tau):
        cand = tau + lax.shift_left(jnp.int32(1), 31 - t)
        cnt = jnp.zeros((QBLK, 1), F32)
        for kc in range(nkc):
            cnt = cnt + jnp.sum(jnp.where(keys_ref[kc] >= cand, 1.0, 0.0), axis=-1, keepdims=True)
        return jnp.where(cnt >= float(topk), cand, tau)

    tau = lax.fori_loop(0, 32, bit_body, jnp.full((QBLK, 1), INT_MIN, jnp.int32))
    tau = jnp.maximum(tau, jnp.int32(INT_MIN + 1))
    for kc in range(nkc):
        bias_ref[:, kc * tk:(kc + 1) * tk] = jnp.where(keys_ref[kc] >= tau, 0.0, NEG_BIG)

    qk_scale = LOG2E * float(qk_dim) ** -0.5
    rank = ckv_ref.shape[1]
    for h in range(heads):
        ql = jnp.dot(q_ref[:, h * qk_dim:(h + 1) * qk_dim], wuk_ref[h], preferred_element_type=F32)
        qlat_ref[h * QBLK:(h + 1) * QBLK, :] = (ql * qk_scale).astype(BF16)

    m_rows = hg * QBLK
    kv = ckv_ref[...]
    bias = bias_ref[...][None]
    for g in range(heads // hg):
        s = lax.dot_general(qlat_ref[g * m_rows:(g + 1) * m_rows, :], kv, contract_last,
                            preferred_element_type=F32)
        s = (s.reshape(hg, QBLK, lk) + bias).reshape(m_rows, lk)
        p = jnp.exp2(s - jnp.max(s, axis=-1, keepdims=True))
        denom = jnp.sum(p, axis=-1, keepdims=True)
        o = (jnp.dot(p.astype(BF16), kv, preferred_element_type=F32) / denom).astype(BF16)
        for r in range(hg):
            h = g * hg + r
            o_ref[:, h * v_dim:(h + 1) * v_dim] = jnp.dot(
                o[r * QBLK:(r + 1) * QBLK, :], wuv_ref[h], preferred_element_type=F32).astype(o_ref.dtype)


def dsa_attention(u3, usm3, ckvn3, wuk, wuv, *, col_q, col_qi, idx_dim, idx_heads, topk, sb_rows, tk, hg):
    bsz, seq, _ = u3.shape
    heads, qk_dim, rank = wuk.shape
    v_dim = wuv.shape[2]
    qi_half = idx_heads * idx_dim // 2
    outs = []
    for sb in range(seq // sb_rows):
        lk = (sb + 1) * sb_rows
        qb0 = sb * (sb_rows // QBLK)
        kern = functools.partial(_dsa_kernel, lk=lk, q0=sb * sb_rows, topk=topk, heads=heads, idx_heads=idx_heads,
                                 idx_dim=idx_dim, qk_dim=qk_dim, v_dim=v_dim, tk=tk, hg=hg)
        outs.append(pl.pallas_call(
            kern,
            grid=(bsz, sb_rows // QBLK),
            in_specs=[pl.BlockSpec((None, QBLK, heads * qk_dim), lambda b, j, qb0=qb0: (b, qb0 + j, col_q)),
                      pl.BlockSpec((None, QBLK, qi_half), lambda b, j, qb0=qb0: (b, qb0 + j, col_qi)),
                      pl.BlockSpec((None, QBLK, qi_half), lambda b, j, qb0=qb0: (b, qb0 + j, col_qi + 1)),
                      pl.BlockSpec((None, QBLK, LANES), lambda b, j, qb0=qb0: (b, qb0 + j, 0)),
                      pl.BlockSpec((None, lk, rank), lambda b, j: (b, 0, 0)),
                      pl.BlockSpec((None, lk, LANES), lambda b, j: (b, 0, 0)),
                      pl.BlockSpec((heads, qk_dim, rank), lambda b, j: (0, 0, 0)),
                      pl.BlockSpec((heads, rank, v_dim), lambda b, j: (0, 0, 0))],
            out_specs=pl.BlockSpec((None, QBLK, heads * v_dim), lambda b, j: (b, j, 0)),
            out_shape=jax.ShapeDtypeStruct((bsz, sb_rows, heads * v_dim), BF16),
            scratch_shapes=[pltpu.VMEM((lk // tk, QBLK, tk), jnp.int32),
                            pltpu.VMEM((QBLK, lk), F32),
                            pltpu.VMEM((heads * QBLK, rank), BF16)],
            compiler_params=_cparams("parallel", "parallel"),
            name=f"dsa_attention_sb{sb}",
        )(u3, u3, u3, usm3, ckvn3, usm3, wuk, wuv))
    return jnp.concatenate(outs, axis=1)


def _softplus(v):
    return jnp.maximum(v, 0.0) + jnp.log1p(jnp.exp(-jnp.abs(v)))


def _split3(v):
    hi = v.astype(BF16)
    r1 = v - hi.astype(F32)
    mid = r1.astype(BF16)
    lo = (r1 - mid.astype(F32)).astype(BF16)
    return hi, mid, lo


def _dot_exact_right(v, m01):
    return sum(jnp.dot(p, m01, preferred_element_type=F32) for p in _split3(v))


def _dot_exact_left(m01, v):
    return sum(jnp.dot(m01, p, preferred_element_type=F32) for p in _split3(v))


def _mamba_kernel(z_ref, xbc_ref, xbcp_ref, dt_ref, dtt_ref, cw_ref, cb_ref, alr_ref, dbr_ref, alc_ref, dbc_ref,
                  dsk_ref, ng_ref, ex_ref, o_ref, hs, xcb, st_ref, ybuf,
                  *, q, taps, heads, hdim, groups, nstate):
    c = pl.program_id(1)
    d_inner = heads * hdim
    gn = groups * nstate
    gw = d_inner // groups
    hpg = heads // groups

    @pl.when(c == 0)
    def _():
        st_ref[...] = jnp.zeros(st_ref.shape, F32)

    hs[0, 0:HALO, :] = jnp.where(c != 0, xbcp_ref[...].astype(F32), 0.0)
    hs[0, HALO:HALO + q, :] = xbc_ref[...].astype(F32)
    _dwconv_rows(hs, cw_ref, cb_ref, xcb, tl=q, taps=taps, post=_silu)

    dt = _softplus(dt_ref[...] + dbr_ref[...])
    a = dt * (-jnp.exp(alr_ref[...]))
    dtt = _softplus(dtt_ref[...] + dbc_ref[...])
    at = dtt * (-jnp.exp(alc_ref[...]))
    ri = lax.broadcasted_iota(jnp.int32, (q, q), 0)
    ci = lax.broadcasted_iota(jnp.int32, (q, q), 1)
    lower = ri >= ci
    a_cum = _dot_exact_left(jnp.where(lower, 1.0, 0.0).astype(BF16), a)
    a_cum_t = _dot_exact_right(at, jnp.where(ri <= ci, 1.0, 0.0).astype(BF16))

    ex = ex_ref[...]
    dt_x = _dot_exact_right(dt, ex)
    acum_x = _dot_exact_right(a_cum, ex)
    alast_x = acum_x[q - 1:q, :]
    xs = xcb[:, 0:d_inner]
    xdt = xs * dt_x
    xdt_b = xdt.astype(BF16)
    xend_b = (xdt * jnp.exp(alast_x - acum_x)).astype(BF16)
    eac_x = jnp.exp(acum_x)
    cdec_x = jnp.exp(alast_x)

    for g in range(groups):
        bg = xcb[:, d_inner + g * nstate:d_inner + (g + 1) * nstate].astype(BF16)
        cg = xcb[:, d_inner + gn + g * nstate:d_inner + gn + (g + 1) * nstate].astype(BF16)
        cbm = lax.dot_general(cg, bg, (((1,), (1,)), ((), ())), preferred_element_type=F32)
        for r in range(hpg):
            h = g * hpg + r
            seg = a_cum[:, h:h + 1] - a_cum_t[h:h + 1, :]
            dec = jnp.exp(jnp.where(lower, seg, NEG_BIG))
            mm = (cbm * dec).astype(BF16)
            ybuf[:, h * hdim:(h + 1) * hdim] = jnp.dot(mm, xdt_b[:, h * hdim:(h + 1) * hdim],
                                                       preferred_element_type=F32)
        lanes = slice(g * gw, (g + 1) * gw)
        st = st_ref[g]
        ybuf[:, lanes] = ybuf[:, lanes] + jnp.dot(cg, st.astype(BF16), preferred_element_type=F32) * eac_x[:, lanes]
        new = lax.dot_general(bg, xend_b[:, lanes], (((0,), (0,)), ((), ())), preferred_element_type=F32)
        st_ref[g] = st * cdec_x[:, lanes] + new

    y = ybuf[...] + xs * dsk_ref[...]
    y = y * _silu(z_ref[...].astype(F32))
    y = y * lax.rsqrt(jnp.mean(y * y, axis=-1, keepdims=True) + EPS) * ng_ref[...]
    o_ref[...] = y.astype(o_ref.dtype)


def mamba2_mixer(u, dts, dtst, col_z, col_xbc, conv_w, conv_b, a_log, dt_bias, d_skip, norm_g,
                 *, bsz, seq, q, heads, hdim, groups, nstate):
    t = u.shape[0]
    d_inner = heads * hdim
    xw = d_inner + 2 * groups * nstate
    taps = conv_w.shape[0]
    nc = seq // q
    hb = q // HALO
    wpad = _tap_weights(conv_w)
    row = lambda v: jnp.zeros((1, LANES), F32).at[0, :heads].set(v)
    col = lambda v: v.reshape(heads, 1).astype(F32)
    expand = (jnp.arange(d_inner)[None, :] // hdim == jnp.arange(LANES)[:, None]).astype(BF16)
    kern = functools.partial(_mamba_kernel, q=q, taps=taps, heads=heads, hdim=hdim, groups=groups, nstate=nstate)
    rowi = lambda b, c: b * nc + c
    full = lambda shp: pl.BlockSpec(shp, lambda b, c: (0,) * len(shp))
    return pl.pallas_call(
        kern,
        grid=(bsz, nc),
        in_specs=[pl.BlockSpec((q, d_inner), lambda b, c: (rowi(b, c), col_z)),
                  pl.BlockSpec((q, xw), lambda b, c: (rowi(b, c), col_xbc)),
                  pl.BlockSpec((HALO, xw), lambda b, c: (jnp.maximum(rowi(b, c) * hb - 1, 0), col_xbc)),
                  pl.BlockSpec((q, LANES), lambda b, c: (rowi(b, c), 0)),
                  pl.BlockSpec((heads, q), lambda b, c: (0, rowi(b, c))),
                  full((taps, SUBLANES, xw)), full((1, xw)),
                  full((1, LANES)), full((1, LANES)), full((heads, 1)), full((heads, 1)),
                  full((1, d_inner)), full((1, d_inner)), full((LANES, d_inner))],
        out_specs=pl.BlockSpec((q, d_inner), lambda b, c: (rowi(b, c), 0)),
        out_shape=jax.ShapeDtypeStruct((t, d_inner), BF16),
        scratch_shapes=[pltpu.VMEM((len(_conv_shifts(taps)), HALO + q, xw), F32),
                        pltpu.VMEM((q, xw), F32),
                        pltpu.VMEM((groups, nstate, d_inner // groups), F32),
                        pltpu.VMEM((q, d_inner), F32)],
        compiler_params=_cparams("parallel", "arbitrary"),
        name="mamba2_ssd",
    )(u, u, u, dts, dtst, wpad, conv_b.reshape(1, xw), row(a_log), row(dt_bias), col(a_log), col(dt_bias),
      jnp.repeat(d_skip.astype(F32), hdim).reshape(1, d_inner), norm_g.reshape(1, d_inner), expand)


ROUTER_ROWS = 8


def _router_kernel(x_ref, rwt_ref, rb_ref, o_ref, *, groups, epg, top_k):
    lg = lax.dot_general(rwt_ref[...], x_ref[...], (((1,), (1,)), ((), ())), preferred_element_type=F32)
    sc = _sigmoid(lg)
    sel = sc + rb_ref[...]
    s = [sel[p * groups:(p + 1) * groups, :] for p in range(epg)]
    w = [sc[p * groups:(p + 1) * groups, :] for p in range(epg)]
    gs = None
    for p in range(epg):
        for r in range(p + 1, epg):
            pair = s[p] + s[r]
            gs = pair if gs is None else jnp.maximum(gs, pair)
    gidx = lax.broadcasted_iota(jnp.int32, gs.shape, 0)
    gmax = jnp.max(gs, axis=0, keepdims=True)
    gbest = jnp.min(jnp.where(gs == gmax, gidx, groups), axis=0, keepdims=True)
    pick = gidx == gbest
    v = [jnp.sum(jnp.where(pick, s[p], 0.0), axis=0, keepdims=True) for p in range(epg)]
    wv = [jnp.sum(jnp.where(pick, w[p], 0.0), axis=0, keepdims=True) for p in range(epg)]
    chosen = []
    for p in range(epg):
        rank = jnp.zeros_like(v[p])
        for r in range(epg):
            if r == p:
                continue
            ahead = (v[r] >= v[p]) if r < p else (v[r] > v[p])
            rank = rank + jnp.where(ahead, 1.0, 0.0)
        chosen.append(jnp.where(rank < float(top_k), wv[p], 0.0))
    tot = chosen[0]
    for p in range(1, epg):
        tot = tot + chosen[p]
    o_ref[...] = jnp.zeros(o_ref.shape, F32)
    for p in range(epg):
        o_ref[p:p + 1, :] = chosen[p] / tot
    o_ref[epg:epg + 1, :] = gbest.astype(F32)


def moe_router(xb, router_w, router_bias, *, groups, epg, top_k, tm):
    t, d = xb.shape
    e = groups * epg
    order = jnp.arange(e).reshape(groups, epg).T.reshape(-1)
    rwt = router_w.T[order].astype(BF16)
    rb = router_bias[order].reshape(e, 1).astype(F32)
    kern = functools.partial(_router_kernel, groups=groups, epg=epg, top_k=top_k)
    return pl.pallas_call(
        kern,
        grid=(t // tm,),
        in_specs=[pl.BlockSpec((tm, d), lambda i: (i, 0)),
                  pl.BlockSpec((e, d), lambda i: (0, 0)),
                  pl.BlockSpec((e, 1), lambda i: (0, 0))],
        out_specs=pl.BlockSpec((ROUTER_ROWS, tm), lambda i: (0, i)),
        out_shape=jax.ShapeDtypeStruct((ROUTER_ROWS, t), F32),
        compiler_params=_cparams("parallel"),
        name="moe_router",
    )(xb, rwt, rb)


def _moe_prep_kernel(g_ref, u_ref, d_ref, go_ref, uo_ref, do_ref):
    go_ref[...] = g_ref[...].astype(BF16)
    uo_ref[...] = u_ref[...].astype(BF16)
    do_ref[...] = d_ref[...].astype(BF16)


def moe_weight_prep(w_gate, w_up, w_down, layer, groups, epg):
    _, e, d, f = w_gate.shape
    dh = d // 2
    up_in = pl.BlockSpec((None, None, dh, f), lambda ei, c: (layer, ei, c, 0))
    up_out = pl.BlockSpec((None, dh, f), lambda ei, c: (ei // epg, c, ei % epg))
    return pl.pallas_call(
        _moe_prep_kernel,
        grid=(e, 2),
        in_specs=[up_in, up_in, pl.BlockSpec((None, None, f, dh), lambda ei, c: (layer, ei, 0, c))],
        out_specs=[up_out, up_out, pl.BlockSpec((None, f, dh), lambda ei, c: (ei // epg, ei % epg, c))],
        out_shape=[jax.ShapeDtypeStruct((groups, d, epg * f), BF16),
                   jax.ShapeDtypeStruct((groups, d, epg * f), BF16),
                   jax.ShapeDtypeStruct((groups, epg * f, d), BF16)],
        compiler_params=_cparams("parallel", "parallel"),
        name="moe_weight_prep",
    )(w_gate, w_up, w_down)


def _moe_up_kernel(tg_ref, x_ref, gate_ref, wg_ref, wu_ref, h_ref, *, epg, dff):
    x = x_ref[...]
    g = jnp.dot(x, wg_ref[...], preferred_element_type=F32)
    u = jnp.dot(x, wu_ref[...], preferred_element_type=F32)
    h = _silu(g) * u
    gates = gate_ref[...]
    for p in range(epg):
        h_ref[:, p * dff:(p + 1) * dff] = (h[:, p * dff:(p + 1) * dff] * gates[:, p:p + 1]).astype(h_ref.dtype)


def _moe_down_kernel(tg_ref, h_ref, wd_ref, x_ref, g_ref, b_ref, o_ref, *, alpha):
    ffn = jnp.dot(h_ref[...], wd_ref[...], preferred_element_type=F32)
    o_ref[...] = _layer_norm_rows(alpha * x_ref[...] + ffn, g_ref[...], b_ref[...])


def moe_experts(xs_b, xs_f, gates_s, tile_group, wg, wu, wd, ln_g, ln_b, *, alpha, epg, tm):
    p_rows, d = xs_b.shape
    f4 = wg.shape[2]
    dff = f4 // epg
    n_tiles = p_rows // tm
    once = pl.Buffered(1)
    h = pl.pallas_call(
        functools.partial(_moe_up_kernel, epg=epg, dff=dff),
        grid_spec=pltpu.PrefetchScalarGridSpec(
            num_scalar_prefetch=1,
            grid=(n_tiles,),
            in_specs=[pl.BlockSpec((tm, d), lambda i, tg: (i, 0)),
                      pl.BlockSpec((tm, epg), lambda i, tg: (i, 0)),
                      pl.BlockSpec((None, d, f4), lambda i, tg: (tg[i], 0, 0), pipeline_mode=once),
                      pl.BlockSpec((None, d, f4), lambda i, tg: (tg[i], 0, 0), pipeline_mode=once)],
            out_specs=pl.BlockSpec((tm, f4), lambda i, tg: (i, 0))),
        out_shape=jax.ShapeDtypeStruct((p_rows, f4), BF16),
        compiler_params=_cparams("arbitrary"),
        name="moe_gate_up",
    )(tile_group, xs_b, gates_s, wg, wu)
    return pl.pallas_call(
        functools.partial(_moe_down_kernel, alpha=alpha),
        grid_spec=pltpu.PrefetchScalarGridSpec(
            num_scalar_prefetch=1,
            grid=(n_tiles,),
            in_specs=[pl.BlockSpec((tm, f4), lambda i, tg: (i, 0)),
                      pl.BlockSpec((None, f4, d), lambda i, tg: (tg[i], 0, 0), pipeline_mode=once),
                      pl.BlockSpec((tm, d), lambda i, tg: (i, 0)),
                      pl.BlockSpec((1, d), lambda i, tg: (0, 0)),
                      pl.BlockSpec((1, d), lambda i, tg: (0, 0))],
            out_specs=pl.BlockSpec((tm, d), lambda i, tg: (i, 0))),
        out_shape=jax.ShapeDtypeStruct((p_rows, d), F32),
        compiler_params=_cparams("arbitrary"),
        name="moe_down_ln",
    )(tile_group, h, wd, xs_f, ln_g, ln_b)


def moe_block(x, xb, router_w, router_bias, wg, wu, wd, ln_g, ln_b, *, alpha, groups, epg, top_k, tm):
    t, d = x.shape
    rt = moe_router(xb, router_w, router_bias, groups=groups, epg=epg, top_k=top_k, tm=512)
    gbest = rt[epg].astype(jnp.int32)
    gates = rt[:epg].T
    onehot = (gbest[:, None] == jnp.arange(groups)[None, :]).astype(jnp.int32)
    counts = jnp.sum(onehot, axis=0)
    padded = ((counts + tm - 1) // tm) * tm
    ends = jnp.cumsum(padded)
    rank = jnp.take_along_axis(jnp.cumsum(onehot, axis=0) - onehot, gbest[:, None], axis=1)[:, 0]
    pos = (ends - padded)[gbest] + rank
    p_rows = t + groups * tm
    src = jnp.zeros((p_rows,), jnp.int32).at[pos].set(jnp.arange(t, dtype=jnp.int32))
    gates_s = jnp.zeros((p_rows, epg), F32).at[pos].set(gates)
    tile_start = jnp.arange(p_rows // tm, dtype=jnp.int32) * tm
    tile_group = jnp.minimum(jnp.sum(tile_start[:, None] >= ends[None, :], axis=1), groups - 1).astype(jnp.int32)
    xs_b = xb[src]
    xs_f = x[src]
    out_s = moe_experts(xs_b, xs_f, gates_s, tile_group, wg, wu, wd, ln_g, ln_b, alpha=alpha, epg=epg, tm=tm)
    x2 = out_s[pos]
    return x2, x2.astype(BF16)


def kernel(x, even_w_in, even_conv_a_w, even_conv_a_b, even_ln_a_g, even_ln_a_b, even_kv_norm_g, even_w_uk, even_w_uv, even_w_out, odd_w_in, odd_conv_c_w, odd_conv_d_w, odd_conv_d_b, odd_a_log, odd_dt_bias, odd_d_skip, odd_norm_g, odd_w_out, ln1_g, ln1_b, ln2_g, ln2_b, router_w, router_bias, moe_w_gate, moe_w_up, moe_w_down):
    bsz, seq, d = x.shape
    t = bsz * seq
    depth = ln1_g.shape[0]
    alpha = (2 * depth) ** 0.25

    a_ch = even_conv_a_w.shape[2]
    heads, qk_dim, rank = even_w_uk.shape[1:]
    v_dim = even_w_uv.shape[3]
    idx_dim, idx_heads = 64, 16
    q_w = heads * qk_dim
    qi_w = idx_heads * idx_dim
    topk = min(256, seq // 4)

    c_ch = odd_conv_c_w.shape[2]
    d_heads = odd_a_log.shape[1]
    d_inner = odd_norm_g.shape[1]
    xw = odd_conv_d_w.shape[2]
    d_groups, d_state, d_chunk = 8, 128, 128

    n_experts, _, dff = moe_w_gate.shape[1:]
    groups, top_k = 8, 2
    epg = n_experts // groups

    xf = x.reshape(t, d)
    xb = xf.astype(BF16)

    def pad_lanes(w):
        return jnp.pad(w, ((0, 0), (0, LANES - w.shape[1])))

    proj_tm, proj_tn = 512, 1536
    for layer in range(depth):
        i = layer // 2
        if layer % 2 == 0:
            o_q = 2 * a_ch
            o_ckv = o_q + q_w
            o_qi = o_ckv + rank
            o_k = o_qi + qi_w
            u = matmul(xb, even_w_in, i, o_k, BF16, proj_tm, proj_tn)
            w_small = pad_lanes(even_w_in[i, :, o_k:]).astype(BF16)
            usm, _ = matmul_small(xb, w_small, w_small.T, 512)
            y_a, ckvn = conformer_conv(u, 0, 1, o_ckv // rank, a_ch, rank,
                                       even_conv_a_w[i], even_conv_a_b[i], even_ln_a_g[i], even_ln_a_b[i],
                                       even_kv_norm_g[i], seq, 256)
            y_b = dsa_attention(u.reshape(bsz, seq, -1), usm.reshape(bsz, seq, LANES), ckvn.reshape(bsz, seq, rank),
                                even_w_uk[i].astype(BF16), even_w_uv[i].astype(BF16),
                                col_q=o_q // q_w, col_qi=o_qi // (qi_w // 2), idx_dim=idx_dim, idx_heads=idx_heads,
                                topk=topk, sb_rows=512, tk=512, hg=4).reshape(t, heads * v_dim)
            w_out = even_w_out[i].astype(BF16)
            y1, y2 = y_a, y_b
        else:
            o_dt = 3 * c_ch + d_inner + xw
            u = matmul(xb, odd_w_in, i, o_dt, BF16, proj_tm, proj_tn)
            w_small = pad_lanes(odd_w_in[i, :, o_dt:]).astype(BF16)
            dts, dtst = matmul_small(xb, w_small, w_small.T, 512)
            y_c = short_gated_conv(u, 0, 1, 2, c_ch, odd_conv_c_w[i], seq, 256)
            y_d = mamba2_mixer(u, dts, dtst, 3 * c_ch // d_inner, (3 * c_ch + d_inner) // xw,
                               odd_conv_d_w[i], odd_conv_d_b[i], odd_a_log[i], odd_dt_bias[i], odd_d_skip[i],
                               odd_norm_g[i], bsz=bsz, seq=seq, q=d_chunk, heads=d_heads, hdim=d_inner // d_heads,
                               groups=d_groups, nstate=d_state)
            w_out = odd_w_out[i].astype(BF16)
            y1, y2 = y_c, y_d
        xf, xb = matmul_residual_ln(y1, y2, w_out, xf, ln1_g[layer].reshape(1, d), ln1_b[layer].reshape(1, d),
                                    alpha, 512, 512)
        wg, wu, wd = moe_weight_prep(moe_w_gate, moe_w_up, moe_w_down, layer, groups, epg)
        xf, xb = moe_block(xf, xb, router_w, router_bias, wg, wu, wd,
                           ln2_g[layer].reshape(1, d), ln2_b[layer].reshape(1, d),
                           alpha=alpha, groups=groups, epg=epg, top_k=top_k, tm=256)
    return xf.reshape(bsz, seq, d)
```

```python
import functools
import math

import jax
import jax.numpy as jnp
from jax import lax
from jax.experimental import pallas as pl
from jax.experimental.pallas import tpu as pltpu

F32 = jnp.float32
BF16 = jnp.bfloat16

VMEM_LIMIT_BYTES = 56 * 1024 * 1024
LANES = 128
HALO = 32
NEG_BIG = -1e30

EPS = 1e-5


def _cparams(*sem):
    return pltpu.CompilerParams(dimension_semantics=sem, vmem_limit_bytes=VMEM_LIMIT_BYTES)


def _sigmoid(v):
    return 1.0 / (1.0 + jnp.exp(-v))


def _silu(v):
    return v * _sigmoid(v)


CONTRACT_LAST = (((1,), (1,)), ((), ()))


def _mm_kernel(x_ref, wt_ref, o_ref, wb_ref):
    @pl.when(pl.program_id(1) == 0)
    def _():
        wb_ref[...] = wt_ref[...].astype(BF16)

    o_ref[...] = lax.dot_general(x_ref[...], wb_ref[...], CONTRACT_LAST,
                                 preferred_element_type=F32).astype(o_ref.dtype)


def matmul(x, wt, layer, n, out_dtype, tm, tn):
    m, k = x.shape
    return pl.pallas_call(
        _mm_kernel,
        grid=(n // tn, m // tm),
        in_specs=[pl.BlockSpec((tm, k), lambda j, i: (i, 0)),
                  pl.BlockSpec((None, tn, k), lambda j, i: (layer, j, 0), pipeline_mode=pl.Buffered(1))],
        out_specs=pl.BlockSpec((tm, tn), lambda j, i: (i, j)),
        out_shape=jax.ShapeDtypeStruct((m, n), out_dtype),
        scratch_shapes=[pltpu.VMEM((tn, k), BF16)],
        compiler_params=_cparams("arbitrary", "arbitrary"),
        name="proj_matmul",
    )(x, wt)


def _mm_small_kernel(x_ref, wt_ref, o_ref, ot_ref):
    xb = x_ref[...]
    wb = wt_ref[...].astype(BF16)
    o_ref[...] = lax.dot_general(xb, wb, CONTRACT_LAST, preferred_element_type=F32)
    ot_ref[...] = lax.dot_general(wb, xb, CONTRACT_LAST, preferred_element_type=F32)


def matmul_small(x, wt, tm):
    m, k = x.shape
    n = wt.shape[0]
    return pl.pallas_call(
        _mm_small_kernel,
        grid=(m // tm,),
        in_specs=[pl.BlockSpec((tm, k), lambda i: (i, 0)),
                  pl.BlockSpec((n, k), lambda i: (0, 0))],
        out_specs=[pl.BlockSpec((tm, n), lambda i: (i, 0)),
                   pl.BlockSpec((n, tm), lambda i: (0, i))],
        out_shape=[jax.ShapeDtypeStruct((m, n), F32), jax.ShapeDtypeStruct((n, m), F32)],
        compiler_params=_cparams("parallel"),
        name="proj_small",
    )(x, wt)


def _layer_norm_rows(v, g, b):
    mu = jnp.mean(v, axis=-1, keepdims=True)
    d = v - mu
    var = jnp.mean(d * d, axis=-1, keepdims=True)
    return d * lax.rsqrt(var + EPS) * g + b


def _mm_ln_kernel(y1_ref, y2_ref, w_ref, x_ref, g_ref, b_ref, o_ref, ob_ref, *, nj, tn, alpha):
    j = pl.program_id(1)
    y = jnp.concatenate([y1_ref[...], y2_ref[...]], axis=1)
    pre = jnp.dot(y, w_ref[...], preferred_element_type=F32) + alpha * x_ref[...]
    o_ref[:, pl.ds(pl.multiple_of(j * tn, tn), tn)] = pre

    @pl.when(j == nj - 1)
    def _():
        v = _layer_norm_rows(o_ref[...], g_ref[...], b_ref[...])
        o_ref[...] = v
        ob_ref[...] = v.astype(BF16)


def matmul_residual_ln(y1, y2, w, x, g, b, alpha, tm, tn):
    m, k1 = y1.shape
    k2 = y2.shape[1]
    n = w.shape[1]
    nj = n // tn
    kern = functools.partial(_mm_ln_kernel, nj=nj, tn=tn, alpha=alpha)
    return pl.pallas_call(
        kern,
        grid=(m // tm, nj),
        in_specs=[pl.BlockSpec((tm, k1), lambda i, j: (i, 0)),
                  pl.BlockSpec((tm, k2), lambda i, j: (i, 0)),
                  pl.BlockSpec((k1 + k2, tn), lambda i, j: (0, j)),
                  pl.BlockSpec((tm, tn), lambda i, j: (i, j)),
                  pl.BlockSpec((1, n), lambda i, j: (0, 0)),
                  pl.BlockSpec((1, n), lambda i, j: (0, 0))],
        out_specs=[pl.BlockSpec((tm, n), lambda i, j: (i, 0)),
                   pl.BlockSpec((tm, n), lambda i, j: (i, 0))],
        out_shape=[jax.ShapeDtypeStruct((m, n), F32), jax.ShapeDtypeStruct((m, n), BF16)],
        compiler_params=_cparams("parallel", "arbitrary"),
        name="outproj_ln",
    )(y1, y2, w, x, g, b)


CONV_ROWS = 32
CONV_LANES = 512


SUBLANES = 8


def _conv_shifts(taps):
    off = HALO - (taps - 1)
    return sorted({(off + k) % SUBLANES for k in range(taps)} | {0})


def _tap_weights(conv_w):
    taps, c = conv_w.shape
    return jnp.broadcast_to(conv_w.astype(F32)[:, None, :], (taps, SUBLANES, c))


def _dwconv_rows(hs, w_ref, b_ref, dst, *, tl, taps, post):
    c = dst.shape[1]
    off = HALO - (taps - 1)
    shifts = _conv_shifts(taps)
    nr = tl + HALO - SUBLANES
    for si, s in enumerate(shifts):
        if s:
            hs[si, 0:nr, :] = hs[0, s:s + nr, :]

    def row_body(r, carry):
        r0 = pl.multiple_of(r * CONV_ROWS, CONV_ROWS)
        for cj in range(c // CONV_LANES):
            cs = slice(cj * CONV_LANES, (cj + 1) * CONV_LANES)
            if b_ref is None:
                acc = jnp.zeros((CONV_ROWS, CONV_LANES), F32)
            else:
                acc = jnp.broadcast_to(b_ref[:, cs], (CONV_ROWS, CONV_LANES))
            for k in range(taps):
                o = off + k
                si = shifts.index(o % SUBLANES)
                wk = jnp.concatenate([w_ref[k, :, cs]] * (CONV_ROWS // SUBLANES), axis=0)
                acc = acc + hs[si, pl.ds(r0 + (o - o % SUBLANES), CONV_ROWS), cs] * wk
            dst[pl.ds(r0, CONV_ROWS), cs] = post(acc).astype(dst.dtype)
        return carry

    lax.fori_loop(0, tl // CONV_ROWS, row_body, 0)


def _conformer_kernel(a_ref, g_ref, ap_ref, gp_ref, ckv_ref, w_ref, cb_ref, lg_ref, lb_ref, kg_ref,
                      o_ref, ckvn_ref, hbuf, cbuf, *, tl, taps, tiles_per_seq):
    i = pl.program_id(0)
    hp = ap_ref[...].astype(F32) * _sigmoid(gp_ref[...].astype(F32))
    hbuf[0, 0:HALO, :] = jnp.where(i % tiles_per_seq != 0, hp, 0.0)
    hbuf[0, HALO:HALO + tl, :] = a_ref[...].astype(F32) * _sigmoid(g_ref[...].astype(F32))
    _dwconv_rows(hbuf, w_ref, cb_ref, cbuf, tl=tl, taps=taps, post=lambda v: v)

    def ln_body(r, carry):
        r0 = pl.multiple_of(r * CONV_ROWS, CONV_ROWS)
        y = _layer_norm_rows(cbuf[pl.ds(r0, CONV_ROWS), :], lg_ref[...], lb_ref[...])
        o_ref[pl.ds(r0, CONV_ROWS), :] = _silu(y).astype(o_ref.dtype)
        return carry

    lax.fori_loop(0, tl // CONV_ROWS, ln_body, 0)

    ck = ckv_ref[...].astype(F32)
    ckvn_ref[...] = (ck * lax.rsqrt(jnp.mean(ck * ck, axis=-1, keepdims=True) + EPS) * kg_ref[...]).astype(ckvn_ref.dtype)


def conformer_conv(u, col_a, col_g, col_ckv, ch, rank, conv_w, conv_b, ln_g, ln_b, kv_g, seq, tl):
    t = u.shape[0]
    taps = conv_w.shape[0]
    wpad = _tap_weights(conv_w)
    hb = tl // HALO
    prev = lambda col: (lambda i: (jnp.maximum(i * hb - 1, 0), col))
    kern = functools.partial(_conformer_kernel, tl=tl, taps=taps, tiles_per_seq=seq // tl)
    vec = lambda n: pl.BlockSpec((1, n), lambda i: (0, 0))
    return pl.pallas_call(
        kern,
        grid=(t // tl,),
        in_specs=[pl.BlockSpec((tl, ch), lambda i: (i, col_a)),
                  pl.BlockSpec((tl, ch), lambda i: (i, col_g)),
                  pl.BlockSpec((HALO, ch), prev(col_a)),
                  pl.BlockSpec((HALO, ch), prev(col_g)),
                  pl.BlockSpec((tl, rank), lambda i: (i, col_ckv)),
                  pl.BlockSpec((taps, SUBLANES, ch), lambda i: (0, 0, 0)),
                  vec(ch), vec(ch), vec(ch), vec(rank)],
        out_specs=[pl.BlockSpec((tl, ch), lambda i: (i, 0)),
                   pl.BlockSpec((tl, rank), lambda i: (i, 0))],
        out_shape=[jax.ShapeDtypeStruct((t, ch), BF16), jax.ShapeDtypeStruct((t, rank), BF16)],
        scratch_shapes=[pltpu.VMEM((len(_conv_shifts(taps)), HALO + tl, ch), F32), pltpu.VMEM((tl, ch), F32)],
        compiler_params=_cparams("parallel"),
        name="conformer_conv",
    )(u, u, u, u, u, wpad, conv_b.reshape(1, ch), ln_g.reshape(1, ch), ln_b.reshape(1, ch), kv_g.reshape(1, rank))


def _short_conv_kernel(bg_ref, cg_ref, h_ref, cgp_ref, hp_ref, w_ref, o_ref, hbuf, cbuf, *, tl, taps, tiles_per_seq):
    i = pl.program_id(0)
    hp = cgp_ref[...].astype(F32) * hp_ref[...].astype(F32)
    hbuf[0, 0:HALO, :] = jnp.where(i % tiles_per_seq != 0, hp, 0.0)
    hbuf[0, HALO:HALO + tl, :] = cg_ref[...].astype(F32) * h_ref[...].astype(F32)
    _dwconv_rows(hbuf, w_ref, None, cbuf, tl=tl, taps=taps, post=lambda v: v)
    o_ref[...] = (bg_ref[...].astype(F32) * cbuf[...]).astype(o_ref.dtype)


def short_gated_conv(u, col_bg, col_cg, col_h, ch, conv_w, seq, tl):
    t = u.shape[0]
    taps = conv_w.shape[0]
    wpad = _tap_weights(conv_w)
    hb = tl // HALO
    prev = lambda col: (lambda i: (jnp.maximum(i * hb - 1, 0), col))
    kern = functools.partial(_short_conv_kernel, tl=tl, taps=taps, tiles_per_seq=seq // tl)
    return pl.pallas_call(
        kern,
        grid=(t // tl,),
        in_specs=[pl.BlockSpec((tl, ch), lambda i: (i, col_bg)),
                  pl.BlockSpec((tl, ch), lambda i: (i, col_cg)),
                  pl.BlockSpec((tl, ch), lambda i: (i, col_h)),
                  pl.BlockSpec((HALO, ch), prev(col_cg)),
                  pl.BlockSpec((HALO, ch), prev(col_h)),
                  pl.BlockSpec((taps, SUBLANES, ch), lambda i: (0, 0, 0))],
        out_specs=pl.BlockSpec((tl, ch), lambda i: (i, 0)),
        out_shape=jax.ShapeDtypeStruct((t, ch), BF16),
        scratch_shapes=[pltpu.VMEM((len(_conv_shifts(taps)), HALO + tl, ch), F32), pltpu.VMEM((tl, ch), F32)],
        compiler_params=_cparams("parallel"),
        name="short_gated_conv",
    )(u, u, u, u, u, wpad)


QBLK = 128
INT_MIN = -(2 ** 31)
LOG2E = 1.4426950408889634


def _dsa_kernel(q_ref, qia_ref, qib_ref, kwq_ref, ckv_ref, kidx_ref, wuk_ref, wuv_ref, o_ref,
                keys_ref, bias_ref, qlat_ref,
                *, lk, q0, topk, heads, idx_heads, idx_dim, qk_dim, v_dim, tk, hg):
    j = pl.program_id(1)
    qbase = q0 + j * QBLK
    nkc = lk // tk
    contract_last = (((1,), (1,)), ((), ()))

    idx_scale = float(idx_heads * idx_dim) ** -0.5
    widx = kwq_ref[:, idx_dim:idx_dim + idx_heads] * idx_scale
    rowpos = qbase + lax.broadcasted_iota(jnp.int32, (QBLK, tk), 0)
    hpb = qia_ref.shape[1] // idx_dim
    for kc in range(nkc):
        kk = kidx_ref[kc * tk:(kc + 1) * tk, 0:idx_dim].astype(BF16)
        sc = jnp.zeros((QBLK, tk), F32)
        for h in range(idx_heads):
            qi_ref, hl = (qia_ref, h) if h < hpb else (qib_ref, h - hpb)
            lg = lax.dot_general(qi_ref[:, hl * idx_dim:(hl + 1) * idx_dim], kk, contract_last,
                                 preferred_element_type=F32)
            sc = sc + jnp.maximum(lg, 0.0) * widx[:, h:h + 1]
        colpos = kc * tk + lax.broadcasted_iota(jnp.int32, (QBLK, tk), 1)
        bits = lax.bitcast_convert_type(sc, jnp.int32)
        key = jnp.where(bits >= 0, bits, bits ^ jnp.int32(0x7FFFFFFF))
        keys_ref[kc] = jnp.where(colpos <= rowpos, key, jnp.int32(INT_MIN))

    def bit_body(t, tau):
        cand = tau + lax.shift_left(jnp.int32(1), 31 - t)
        cnt = jnp.zeros((QBLK, 1), F32)
        for kc in range(nkc):
            cnt = cnt + jnp.sum(jnp.where(keys_ref[kc] >= cand, 1.0, 0.0), axis=-1, keepdims=True)
        return jnp.where(cnt >= float(topk), cand, tau)

    tau = lax.fori_loop(0, 32, bit_body, jnp.full((QBLK, 1), INT_MIN, jnp.int32))
    tau = jnp.maximum(tau, jnp.int32(INT_MIN + 1))
    for kc in range(nkc):
        bias_ref[:, kc * tk:(kc + 1) * tk] = jnp.where(keys_ref[kc] >= tau, 0.0, NEG_BIG)

    qk_scale = LOG2E * float(qk_dim) ** -0.5
    rank = ckv_ref.shape[1]
    for h in range(heads):
        ql = jnp.dot(q_ref[:, h * qk_dim:(h + 1) * qk_dim], wuk_ref[h], preferred_element_type=F32)
        qlat_ref[h * QBLK:(h + 1) * QBLK, :] = (ql * qk_scale).astype(BF16)

    m_rows = hg * QBLK
    kv = ckv_ref[...]
    bias = bias_ref[...][None]
    for g in range(heads // hg):
        s = lax.dot_general(qlat_ref[g * m_rows:(g + 1) * m_rows, :], kv, contract_last,
                            preferred_element_type=F32)
        s = (s.reshape(hg, QBLK, lk) + bias).reshape(m_rows, lk)
        p = jnp.exp2(s - jnp.max(s, axis=-1, keepdims=True))
        denom = jnp.sum(p, axis=-1, keepdims=True)
        o = (jnp.dot(p.astype(BF16), kv, preferred_element_type=F32) / denom).astype(BF16)
        for r in range(hg):
            h = g * hg + r
            o_ref[:, h * v_dim:(h + 1) * v_dim] = jnp.dot(
                o[r * QBLK:(r + 1) * QBLK, :], wuv_ref[h], preferred_element_type=F32).astype(o_ref.dtype)


def dsa_attention(u3, usm3, ckvn3, wuk, wuv, *, col_q, col_qi, idx_dim, idx_heads, topk, sb_rows, tk, hg):
    bsz, seq, _ = u3.shape
    heads, qk_dim, rank = wuk.shape
    v_dim = wuv.shape[2]
    qi_half = idx_heads * idx_dim // 2
    outs = []
    for sb in range(seq // sb_rows):
        lk = (sb + 1) * sb_rows
        qb0 = sb * (sb_rows // QBLK)
        kern = functools.partial(_dsa_kernel, lk=lk, q0=sb * sb_rows, topk=topk, heads=heads, idx_heads=idx_heads,
                                 idx_dim=idx_dim, qk_dim=qk_dim, v_dim=v_dim, tk=tk, hg=hg)
        outs.append(pl.pallas_call(
            kern,
            grid=(bsz, sb_rows // QBLK),
            in_specs=[pl.BlockSpec((None, QBLK, heads * qk_dim), lambda b, j, qb0=qb0: (b, qb0 + j, col_q)),
                      pl.BlockSpec((None, QBLK, qi_half), lambda b, j, qb0=qb0: (b, qb0 + j, col_qi)),
                      pl.BlockSpec((None, QBLK, qi_half), lambda b, j, qb0=qb0: (b, qb0 + j, col_qi + 1)),
                      pl.BlockSpec((None, QBLK, LANES), lambda b, j, qb0=qb0: (b, qb0 + j, 0)),
                      pl.BlockSpec((None, lk, rank), lambda b, j: (b, 0, 0)),
                      pl.BlockSpec((None, lk, LANES), lambda b, j: (b, 0, 0)),
                      pl.BlockSpec((heads, qk_dim, rank), lambda b, j: (0, 0, 0)),
                      pl.BlockSpec((heads, rank, v_dim), lambda b, j: (0, 0, 0))],
            out_specs=pl.BlockSpec((None, QBLK, heads * v_dim), lambda b, j: (b, j, 0)),
            out_shape=jax.ShapeDtypeStruct((bsz, sb_rows, heads * v_dim), BF16),
            scratch_shapes=[pltpu.VMEM((lk // tk, QBLK, tk), jnp.int32),
                            pltpu.VMEM((QBLK, lk), F32),
                            pltpu.VMEM((heads * QBLK, rank), BF16)],
            compiler_params=_cparams("parallel", "parallel"),
            name=f"dsa_attention_sb{sb}",
        )(u3, u3, u3, usm3, ckvn3, usm3, wuk, wuv))
    return jnp.concatenate(outs, axis=1)


def _softplus(v):
    return jnp.maximum(v, 0.0) + jnp.log1p(jnp.exp(-jnp.abs(v)))


def _split3(v):
    hi = v.astype(BF16)
    r1 = v - hi.astype(F32)
    mid = r1.astype(BF16)
    lo = (r1 - mid.astype(F32)).astype(BF16)
    return hi, mid, lo


def _dot_exact_right(v, m01):
    return sum(jnp.dot(p, m01, preferred_element_type=F32) for p in _split3(v))


def _dot_exact_left(m01, v):
    return sum(jnp.dot(m01, p, preferred_element_type=F32) for p in _split3(v))


def _mamba_kernel(z_ref, xbc_ref, xbcp_ref, dt_ref, dtt_ref, cw_ref, cb_ref, alr_ref, dbr_ref, alc_ref, dbc_ref,
                  dsk_ref, ng_ref, ex_ref, o_ref, hs, xcb, st_ref, ybuf,
                  *, q, taps, heads, hdim, groups, nstate):
    c = pl.program_id(1)
    d_inner = heads * hdim
    gn = groups * nstate
    gw = d_inner // groups
    hpg = heads // groups

    @pl.when(c == 0)
    def _():
        st_ref[...] = jnp.zeros(st_ref.shape, F32)

    hs[0, 0:HALO, :] = jnp.where(c != 0, xbcp_ref[...].astype(F32), 0.0)
    hs[0, HALO:HALO + q, :] = xbc_ref[...].astype(F32)
    _dwconv_rows(hs, cw_ref, cb_ref, xcb, tl=q, taps=taps, post=_silu)

    dt = _softplus(dt_ref[...] + dbr_ref[...])
    a = dt * (-jnp.exp(alr_ref[...]))
    dtt = _softplus(dtt_ref[...] + dbc_ref[...])
    at = dtt * (-jnp.exp(alc_ref[...]))
    ri = lax.broadcasted_iota(jnp.int32, (q, q), 0)
    ci = lax.broadcasted_iota(jnp.int32, (q, q), 1)
    lower = ri >= ci
    a_cum = _dot_exact_left(jnp.where(lower, 1.0, 0.0).astype(BF16), a)
    a_cum_t = _dot_exact_right(at, jnp.where(ri <= ci, 1.0, 0.0).astype(BF16))

    ex = ex_ref[...]
    dt_x = _dot_exact_right(dt, ex)
    acum_x = _dot_exact_right(a_cum, ex)
    alast_x = acum_x[q - 1:q, :]
    xs = xcb[:, 0:d_inner]
    xdt = xs * dt_x
    xdt_b = xdt.astype(BF16)
    xend_b = (xdt * jnp.exp(alast_x - acum_x)).astype(BF16)
    eac_x = jnp.exp(acum_x)
    cdec_x = jnp.exp(alast_x)

    for g in range(groups):
        bg = xcb[:, d_inner + g * nstate:d_inner + (g + 1) * nstate].astype(BF16)
        cg = xcb[:, d_inner + gn + g * nstate:d_inner + gn + (g + 1) * nstate].astype(BF16)
        cbm = lax.dot_general(cg, bg, (((1,), (1,)), ((), ())), preferred_element_type=F32)
        for r in range(hpg):
            h = g * hpg + r
            seg = a_cum[:, h:h + 1] - a_cum_t[h:h + 1, :]
            dec = jnp.exp(jnp.where(lower, seg, NEG_BIG))
            mm = (cbm * dec).astype(BF16)
            ybuf[:, h * hdim:(h + 1) * hdim] = jnp.dot(mm, xdt_b[:, h * hdim:(h + 1) * hdim],
                                                       preferred_element_type=F32)
        lanes = slice(g * gw, (g + 1) * gw)
        st = st_ref[g]
        ybuf[:, lanes] = ybuf[:, lanes] + jnp.dot(cg, st.astype(BF16), preferred_element_type=F32) * eac_x[:, lanes]
        new = lax.dot_general(bg, xend_b[:, lanes], (((0,), (0,)), ((), ())), preferred_element_type=F32)
        st_ref[g] = st * cdec_x[:, lanes] + new

    y = ybuf[...] + xs * dsk_ref[...]
    y = y * _silu(z_ref[...].astype(F32))
    y = y * lax.rsqrt(jnp.mean(y * y, axis=-1, keepdims=True) + EPS) * ng_ref[...]
    o_ref[...] = y.astype(o_ref.dtype)


def mamba2_mixer(u, dts, dtst, col_z, col_xbc, conv_w, conv_b, a_log, dt_bias, d_skip, norm_g,
                 *, bsz, seq, q, heads, hdim, groups, nstate):
    t = u.shape[0]
    d_inner = heads * hdim
    xw = d_inner + 2 * groups * nstate
    taps = conv_w.shape[0]
    nc = seq // q
    hb = q // HALO
    wpad = _tap_weights(conv_w)
    row = lambda v: jnp.zeros((1, LANES), F32).at[0, :heads].set(v)
    col = lambda v: v.reshape(heads, 1).astype(F32)
    expand = (jnp.arange(d_inner)[None, :] // hdim == jnp.arange(LANES)[:, None]).astype(BF16)
    kern = functools.partial(_mamba_kernel, q=q, taps=taps, heads=heads, hdim=hdim, groups=groups, nstate=nstate)
    rowi = lambda b, c: b * nc + c
    full = lambda shp: pl.BlockSpec(shp, lambda b, c: (0,) * len(shp))
    return pl.pallas_call(
        kern,
        grid=(bsz, nc),
        in_specs=[pl.BlockSpec((q, d_inner), lambda b, c: (rowi(b, c), col_z)),
                  pl.BlockSpec((q, xw), lambda b, c: (rowi(b, c), col_xbc)),
                  pl.BlockSpec((HALO, xw), lambda b, c: (jnp.maximum(rowi(b, c) * hb - 1, 0), col_xbc)),
                  pl.BlockSpec((q, LANES), lambda b, c: (rowi(b, c), 0)),
                  pl.BlockSpec((heads, q), lambda b, c: (0, rowi(b, c))),
                  full((taps, SUBLANES, xw)), full((1, xw)),
                  full((1, LANES)), full((1, LANES)), full((heads, 1)), full((heads, 1)),
                  full((1, d_inner)), full((1, d_inner)), full((LANES, d_inner))],
        out_specs=pl.BlockSpec((q, d_inner), lambda b, c: (rowi(b, c), 0)),
        out_shape=jax.ShapeDtypeStruct((t, d_inner), BF16),
        scratch_shapes=[pltpu.VMEM((len(_conv_shifts(taps)), HALO + q, xw), F32),
                        pltpu.VMEM((q, xw), F32),
                        pltpu.VMEM((groups, nstate, d_inner // groups), F32),
                        pltpu.VMEM((q, d_inner), F32)],
        compiler_params=_cparams("parallel", "arbitrary"),
        name="mamba2_ssd",
    )(u, u, u, dts, dtst, wpad, conv_b.reshape(1, xw), row(a_log), row(dt_bias), col(a_log), col(dt_bias),
      jnp.repeat(d_skip.astype(F32), hdim).reshape(1, d_inner), norm_g.reshape(1, d_inner), expand)


ROUTER_ROWS = 8


def _router_kernel(x_ref, rwt_ref, rb_ref, o_ref, *, groups, epg, top_k):
    lg = lax.dot_general(rwt_ref[...], x_ref[...], (((1,), (1,)), ((), ())), preferred_element_type=F32)
    sc = _sigmoid(lg)
    sel = sc + rb_ref[...]
    s = [sel[p * groups:(p + 1) * groups, :] for p in range(epg)]
    w = [sc[p * groups:(p + 1) * groups, :] for p in range(epg)]
    gs = None
    for p in range(epg):
        for r in range(p + 1, epg):
            pair = s[p] + s[r]
            gs = pair if gs is None else jnp.maximum(gs, pair)
    gidx = lax.broadcasted_iota(jnp.int32, gs.shape, 0)
    gmax = jnp.max(gs, axis=0, keepdims=True)
    gbest = jnp.min(jnp.where(gs == gmax, gidx, groups), axis=0, keepdims=True)
    pick = gidx == gbest
    v = [jnp.sum(jnp.where(pick, s[p], 0.0), axis=0, keepdims=True) for p in range(epg)]
    wv = [jnp.sum(jnp.where(pick, w[p], 0.0), axis=0, keepdims=True) for p in range(epg)]
    chosen = []
    for p in range(epg):
        rank = jnp.zeros_like(v[p])
        for r in range(epg):
            if r == p:
                continue
            ahead = (v[r] >= v[p]) if r < p else (v[r] > v[p])
            rank = rank + jnp.where(ahead, 1.0, 0.0)
        chosen.append(rank < float(top_k))
    p_lo, w_lo, p_hi, w_hi = (jnp.zeros_like(v[0]) for _ in range(4))
    for p in reversed(range(epg)):
        p_lo = jnp.where(chosen[p], float(p), p_lo)
        w_lo = jnp.where(chosen[p], wv[p], w_lo)
    for p in range(epg):
        p_hi = jnp.where(chosen[p], float(p), p_hi)
        w_hi = jnp.where(chosen[p], wv[p], w_hi)
    tot = w_lo + w_hi
    first_expert = gbest.astype(F32) * float(epg)
    o_ref[...] = jnp.zeros(o_ref.shape, F32)
    o_ref[0:1, :] = first_expert + p_lo
    o_ref[1:2, :] = first_expert + p_hi
    o_ref[2:3, :] = w_lo / tot
    o_ref[3:4, :] = w_hi / tot


def moe_router(xb, router_w, router_bias, *, groups, epg, top_k, tm):
    t, d = xb.shape
    e = groups * epg
    order = jnp.arange(e).reshape(groups, epg).T.reshape(-1)
    rwt = router_w.T[order].astype(BF16)
    rb = router_bias[order].reshape(e, 1).astype(F32)
    kern = functools.partial(_router_kernel, groups=groups, epg=epg, top_k=top_k)
    return pl.pallas_call(
        kern,
        grid=(t // tm,),
        in_specs=[pl.BlockSpec((tm, d), lambda i: (i, 0)),
                  pl.BlockSpec((e, d), lambda i: (0, 0)),
                  pl.BlockSpec((e, 1), lambda i: (0, 0))],
        out_specs=pl.BlockSpec((ROUTER_ROWS, tm), lambda i: (0, i)),
        out_shape=jax.ShapeDtypeStruct((ROUTER_ROWS, t), F32),
        compiler_params=_cparams("parallel"),
        name="moe_router",
    )(xb, rwt, rb)


def _expert_up_kernel(te_ref, first_ref, x_ref, gate_ref, wg_ref, wu_ref, h_ref, wgb_ref, wub_ref):
    @pl.when(first_ref[pl.program_id(0)] == 1)
    def _():
        wgb_ref[...] = wg_ref[...].astype(BF16)
        wub_ref[...] = wu_ref[...].astype(BF16)

    x = x_ref[...]
    g = jnp.dot(x, wgb_ref[...], preferred_element_type=F32)
    u = jnp.dot(x, wub_ref[...], preferred_element_type=F32)
    h_ref[...] = (_silu(g) * u * gate_ref[...]).astype(h_ref.dtype)


def _expert_down_kernel(te_ref, first_ref, h_ref, wd_ref, y_ref, wdb_ref):
    @pl.when(first_ref[pl.program_id(0)] == 1)
    def _():
        wdb_ref[...] = wd_ref[...].astype(BF16)

    y_ref[...] = jnp.dot(h_ref[...], wdb_ref[...], preferred_element_type=F32).astype(y_ref.dtype)


def _combine_ln_kernel(x_ref, ya_ref, yb_ref, g_ref, b_ref, o_ref, ob_ref, *, alpha):
    ffn = ya_ref[...].astype(F32) + yb_ref[...].astype(F32)
    v = _layer_norm_rows(alpha * x_ref[...] + ffn, g_ref[...], b_ref[...])
    o_ref[...] = v
    ob_ref[...] = v.astype(BF16)


def moe_block(x, xb, router_w, router_bias, w_gate, w_up, w_down, layer, ln_g, ln_b,
              *, alpha, groups, epg, top_k, tm):
    assert top_k == 2, "the router reports exactly two experts per token"
    t, d = x.shape
    n_exp, _, dff = w_gate.shape[1:]
    rt = moe_router(xb, router_w, router_bias, groups=groups, epg=epg, top_k=top_k, tm=512)
    e_lo = rt[0].astype(jnp.int32)[:, None]
    e_hi = rt[1].astype(jnp.int32)[:, None]
    g_lo, g_hi = rt[2], rt[3]
    e_ids = jnp.arange(n_exp, dtype=jnp.int32)[None, :]
    is_lo, is_hi = e_ids == e_lo, e_ids == e_hi
    onehot = (is_lo | is_hi).astype(jnp.int32)
    counts = jnp.sum(onehot, axis=0)
    padded = ((counts + tm - 1) // tm) * tm
    ends = jnp.cumsum(padded)
    posmat = (ends - padded)[None, :] + jnp.cumsum(onehot, axis=0) - onehot
    p_rows = top_k * t + n_exp * tm
    pos_lo = jnp.sum(jnp.where(is_lo, posmat, 0), axis=1)
    pos_hi = jnp.sum(jnp.where(is_hi, posmat, 0), axis=1)
    tok = jnp.arange(t, dtype=jnp.int32)
    pos2 = jnp.concatenate([pos_lo, pos_hi])
    upd = jnp.stack([jnp.concatenate([tok, tok]),
                     lax.bitcast_convert_type(jnp.concatenate([g_lo, g_hi]), jnp.int32)], axis=1)
    rows = jnp.zeros((p_rows, 2), jnp.int32).at[pos2].set(upd)
    src = rows[:, 0]
    gate_s = lax.bitcast_convert_type(rows[:, 1], F32).reshape(p_rows, 1)
    n_tiles = p_rows // tm
    tile_start = jnp.arange(n_tiles, dtype=jnp.int32) * tm
    tile_exp = jnp.minimum(jnp.sum(tile_start[:, None] >= ends[None, :], axis=1), n_exp - 1).astype(jnp.int32)
    first = jnp.concatenate([jnp.ones((1,), jnp.int32), (tile_exp[1:] != tile_exp[:-1]).astype(jnp.int32)])

    xs = xb[src]
    h = pl.pallas_call(
        _expert_up_kernel,
        grid_spec=pltpu.PrefetchScalarGridSpec(
            num_scalar_prefetch=2,
            grid=(n_tiles,),
            in_specs=[pl.BlockSpec((tm, d), lambda i, te, fi: (i, 0)),
                      pl.BlockSpec((tm, 1), lambda i, te, fi: (i, 0)),
                      pl.BlockSpec((None, None, d, dff), lambda i, te, fi: (layer, te[i], 0, 0)),
                      pl.BlockSpec((None, None, d, dff), lambda i, te, fi: (layer, te[i], 0, 0))],
            out_specs=pl.BlockSpec((tm, dff), lambda i, te, fi: (i, 0)),
            scratch_shapes=[pltpu.VMEM((d, dff), BF16), pltpu.VMEM((d, dff), BF16)]),
        out_shape=jax.ShapeDtypeStruct((p_rows, dff), BF16),
        compiler_params=_cparams("arbitrary"),
        name="moe_expert_up",
    )(tile_exp, first, xs, gate_s, w_gate, w_up)
    y = pl.pallas_call(
        _expert_down_kernel,
        grid_spec=pltpu.PrefetchScalarGridSpec(
            num_scalar_prefetch=2,
            grid=(n_tiles,),
            in_specs=[pl.BlockSpec((tm, dff), lambda i, te, fi: (i, 0)),
                      pl.BlockSpec((None, None, dff, d), lambda i, te, fi: (layer, te[i], 0, 0))],
            out_specs=pl.BlockSpec((tm, d), lambda i, te, fi: (i, 0)),
            scratch_shapes=[pltpu.VMEM((dff, d), BF16)]),
        out_shape=jax.ShapeDtypeStruct((p_rows, d), BF16),
        compiler_params=_cparams("arbitrary"),
        name="moe_expert_down",
    )(tile_exp, first, h, w_down)
    ya = y[pos_lo]
    yb = y[pos_hi]
    row = pl.BlockSpec((tm, d), lambda i: (i, 0))
    vec = pl.BlockSpec((1, d), lambda i: (0, 0))
    return pl.pallas_call(
        functools.partial(_combine_ln_kernel, alpha=alpha),
        grid=(t // tm,),
        in_specs=[row, row, row, vec, vec],
        out_specs=[row, row],
        out_shape=[jax.ShapeDtypeStruct((t, d), F32), jax.ShapeDtypeStruct((t, d), BF16)],
        compiler_params=_cparams("parallel"),
        name="moe_combine_ln",
    )(x, ya, yb, ln_g, ln_b)


def kernel(x, even_w_in, even_conv_a_w, even_conv_a_b, even_ln_a_g, even_ln_a_b, even_kv_norm_g, even_w_uk, even_w_uv, even_w_out, odd_w_in, odd_conv_c_w, odd_conv_d_w, odd_conv_d_b, odd_a_log, odd_dt_bias, odd_d_skip, odd_norm_g, odd_w_out, ln1_g, ln1_b, ln2_g, ln2_b, router_w, router_bias, moe_w_gate, moe_w_up, moe_w_down):
    bsz, seq, d = x.shape
    t = bsz * seq
    depth = ln1_g.shape[0]
    alpha = (2 * depth) ** 0.25

    a_ch = even_conv_a_w.shape[2]
    heads, qk_dim, rank = even_w_uk.shape[1:]
    v_dim = even_w_uv.shape[3]
    idx_dim, idx_heads = 64, 16
    q_w = heads * qk_dim
    qi_w = idx_heads * idx_dim
    topk = min(256, seq // 4)

    c_ch = odd_conv_c_w.shape[2]
    d_heads = odd_a_log.shape[1]
    d_inner = odd_norm_g.shape[1]
    xw = odd_conv_d_w.shape[2]
    d_groups, d_state, d_chunk = 8, 128, 128

    n_experts, _, dff = moe_w_gate.shape[1:]
    groups, top_k = 8, 2
    epg = n_experts // groups

    xf = x.reshape(t, d)
    xb = xf.astype(BF16)

    def pad_rows(w):
        return jnp.pad(w, ((0, LANES - w.shape[0]), (0, 0)))

    even_wt = jnp.swapaxes(even_w_in, 1, 2)
    odd_wt = jnp.swapaxes(odd_w_in, 1, 2)
    proj_tm, proj_tn = 512, 1536
    for layer in range(depth):
        i = layer // 2
        if layer % 2 == 0:
            o_q = 2 * a_ch
            o_ckv = o_q + q_w
            o_qi = o_ckv + rank
            o_k = o_qi + qi_w
            u = matmul(xb, even_wt, i, o_k, BF16, proj_tm, proj_tn)
            usm, _ = matmul_small(xb, pad_rows(even_wt[i, o_k:, :]), 512)
            y_a, ckvn = conformer_conv(u, 0, 1, o_ckv // rank, a_ch, rank,
                                       even_conv_a_w[i], even_conv_a_b[i], even_ln_a_g[i], even_ln_a_b[i],
                                       even_kv_norm_g[i], seq, 256)
            y_b = dsa_attention(u.reshape(bsz, seq, -1), usm.reshape(bsz, seq, LANES), ckvn.reshape(bsz, seq, rank),
                                even_w_uk[i].astype(BF16), even_w_uv[i].astype(BF16),
                                col_q=o_q // q_w, col_qi=o_qi // (qi_w // 2), idx_dim=idx_dim, idx_heads=idx_heads,
                                topk=topk, sb_rows=512, tk=512, hg=4).reshape(t, heads * v_dim)
            w_out = even_w_out[i].astype(BF16)
            y1, y2 = y_a, y_b
        else:
            o_dt = 3 * c_ch + d_inner + xw
            u = matmul(xb, odd_wt, i, o_dt, BF16, proj_tm, proj_tn)
            dts, dtst = matmul_small(xb, pad_rows(odd_wt[i, o_dt:, :]), 512)
            y_c = short_gated_conv(u, 0, 1, 2, c_ch, odd_conv_c_w[i], seq, 256)
            y_d = mamba2_mixer(u, dts, dtst, 3 * c_ch // d_inner, (3 * c_ch + d_inner) // xw,
                               odd_conv_d_w[i], odd_conv_d_b[i], odd_a_log[i], odd_dt_bias[i], odd_d_skip[i],
                               odd_norm_g[i], bsz=bsz, seq=seq, q=d_chunk, heads=d_heads, hdim=d_inner // d_heads,
                               groups=d_groups, nstate=d_state)
            w_out = odd_w_out[i].astype(BF16)
            y1, y2 = y_c, y_d
        xf, xb = matmul_residual_ln(y1, y2, w_out, xf, ln1_g[layer].reshape(1, d), ln1_b[layer].reshape(1, d),
                                    alpha, 512, 512)
        xf, xb = moe_block(xf, xb, router_w, router_bias, moe_w_gate, moe_w_up, moe_w_down, layer,
                           ln2_g[layer].reshape(1, d), ln2_b[layer].reshape(1, d),
                           alpha=alpha, groups=groups, epg=epg, top_k=top_k, tm=256)
    return xf.reshape(bsz, seq, d)
```

```python
import functools
import math

import jax
import jax.numpy as jnp
from jax import lax
from jax.experimental import pallas as pl
from jax.experimental.pallas import tpu as pltpu

F32 = jnp.float32
BF16 = jnp.bfloat16

VMEM_LIMIT_BYTES = 56 * 1024 * 1024
LANES = 128
HALO = 32
NEG_BIG = -1e30

EPS = 1e-5


def _cparams(*sem):
    return pltpu.CompilerParams(dimension_semantics=sem, vmem_limit_bytes=VMEM_LIMIT_BYTES)


def _sigmoid(v):
    return 1.0 / (1.0 + jnp.exp(-v))


def _silu(v):
    return v * _sigmoid(v)


CONTRACT_LAST = (((1,), (1,)), ((), ()))


def _mm_kernel(x_ref, wt_ref, o_ref, wb_ref):
    @pl.when(pl.program_id(1) == 0)
    def _():
        wb_ref[...] = wt_ref[...].astype(BF16)

    o_ref[...] = lax.dot_general(x_ref[...], wb_ref[...], CONTRACT_LAST,
                                 preferred_element_type=F32).astype(o_ref.dtype)


def matmul(x, wt, layer, n, out_dtype, tm, tn):
    m, k = x.shape
    return pl.pallas_call(
        _mm_kernel,
        grid=(n // tn, m // tm),
        in_specs=[pl.BlockSpec((tm, k), lambda j, i: (i, 0)),
                  pl.BlockSpec((None, tn, k), lambda j, i: (layer, j, 0), pipeline_mode=pl.Buffered(1))],
        out_specs=pl.BlockSpec((tm, tn), lambda j, i: (i, j)),
        out_shape=jax.ShapeDtypeStruct((m, n), out_dtype),
        scratch_shapes=[pltpu.VMEM((tn, k), BF16)],
        compiler_params=_cparams("arbitrary", "arbitrary"),
        name="proj_matmul",
    )(x, wt)


def _mm_small_kernel(x_ref, wt_ref, o_ref, ot_ref):
    xb = x_ref[...]
    wb = wt_ref[...].astype(BF16)
    o_ref[...] = lax.dot_general(xb, wb, CONTRACT_LAST, preferred_element_type=F32)
    ot_ref[...] = lax.dot_general(wb, xb, CONTRACT_LAST, preferred_element_type=F32)


def matmul_small(x, wt, tm):
    m, k = x.shape
    n = wt.shape[0]
    return pl.pallas_call(
        _mm_small_kernel,
        grid=(m // tm,),
        in_specs=[pl.BlockSpec((tm, k), lambda i: (i, 0)),
                  pl.BlockSpec((n, k), lambda i: (0, 0))],
        out_specs=[pl.BlockSpec((tm, n), lambda i: (i, 0)),
                   pl.BlockSpec((n, tm), lambda i: (0, i))],
        out_shape=[jax.ShapeDtypeStruct((m, n), F32), jax.ShapeDtypeStruct((n, m), F32)],
        compiler_params=_cparams("parallel"),
        name="proj_small",
    )(x, wt)


def _layer_norm_rows(v, g, b):
    mu = jnp.mean(v, axis=-1, keepdims=True)
    d = v - mu
    var = jnp.mean(d * d, axis=-1, keepdims=True)
    return d * lax.rsqrt(var + EPS) * g + b


def _mm_ln_kernel(y1_ref, y2_ref, w_ref, x_ref, g_ref, b_ref, o_ref, ob_ref, *, nj, tn, alpha):
    j = pl.program_id(1)
    y = jnp.concatenate([y1_ref[...], y2_ref[...]], axis=1)
    pre = jnp.dot(y, w_ref[...], preferred_element_type=F32) + alpha * x_ref[...]
    o_ref[:, pl.ds(pl.multiple_of(j * tn, tn), tn)] = pre

    @pl.when(j == nj - 1)
    def _():
        v = _layer_norm_rows(o_ref[...], g_ref[...], b_ref[...])
        o_ref[...] = v
        ob_ref[...] = v.astype(BF16)


def matmul_residual_ln(y1, y2, w, x, g, b, alpha, tm, tn):
    m, k1 = y1.shape
    k2 = y2.shape[1]
    n = w.shape[1]
    nj = n // tn
    kern = functools.partial(_mm_ln_kernel, nj=nj, tn=tn, alpha=alpha)
    return pl.pallas_call(
        kern,
        grid=(m // tm, nj),
        in_specs=[pl.BlockSpec((tm, k1), lambda i, j: (i, 0)),
                  pl.BlockSpec((tm, k2), lambda i, j: (i, 0)),
                  pl.BlockSpec((k1 + k2, tn), lambda i, j: (0, j)),
                  pl.BlockSpec((tm, tn), lambda i, j: (i, j)),
                  pl.BlockSpec((1, n), lambda i, j: (0, 0)),
                  pl.BlockSpec((1, n), lambda i, j: (0, 0))],
        out_specs=[pl.BlockSpec((tm, n), lambda i, j: (i, 0)),
                   pl.BlockSpec((tm, n), lambda i, j: (i, 0))],
        out_shape=[jax.ShapeDtypeStruct((m, n), F32), jax.ShapeDtypeStruct((m, n), BF16)],
        compiler_params=_cparams("parallel", "arbitrary"),
        name="outproj_ln",
    )(y1, y2, w, x, g, b)


CONV_ROWS = 32
CONV_LANES = 512


SUBLANES = 8


def _conv_shifts(taps):
    off = HALO - (taps - 1)
    return sorted({(off + k) % SUBLANES for k in range(taps)} | {0})


def _tap_weights(conv_w):
    taps, c = conv_w.shape
    return jnp.broadcast_to(conv_w.astype(F32)[:, None, :], (taps, SUBLANES, c))


def _dwconv_rows(hs, w_ref, b_ref, dst, *, tl, taps, post):
    c = dst.shape[1]
    off = HALO - (taps - 1)
    shifts = _conv_shifts(taps)
    nr = tl + HALO - SUBLANES
    for si, s in enumerate(shifts):
        if s:
            hs[si, 0:nr, :] = hs[0, s:s + nr, :]

    def row_body(r, carry):
        r0 = pl.multiple_of(r * CONV_ROWS, CONV_ROWS)
        for cj in range(c // CONV_LANES):
            cs = slice(cj * CONV_LANES, (cj + 1) * CONV_LANES)
            if b_ref is None:
                acc = jnp.zeros((CONV_ROWS, CONV_LANES), F32)
            else:
                acc = jnp.broadcast_to(b_ref[:, cs], (CONV_ROWS, CONV_LANES))
            for k in range(taps):
                o = off + k
                si = shifts.index(o % SUBLANES)
                wk = jnp.concatenate([w_ref[k, :, cs]] * (CONV_ROWS // SUBLANES), axis=0)
                acc = acc + hs[si, pl.ds(r0 + (o - o % SUBLANES), CONV_ROWS), cs] * wk
            dst[pl.ds(r0, CONV_ROWS), cs] = post(acc).astype(dst.dtype)
        return carry

    lax.fori_loop(0, tl // CONV_ROWS, row_body, 0)


def _conformer_kernel(a_ref, g_ref, ap_ref, gp_ref, ckv_ref, w_ref, cb_ref, lg_ref, lb_ref, kg_ref,
                      o_ref, ckvn_ref, hbuf, cbuf, *, tl, taps, tiles_per_seq):
    i = pl.program_id(0)
    hp = ap_ref[...].astype(F32) * _sigmoid(gp_ref[...].astype(F32))
    hbuf[0, 0:HALO, :] = jnp.where(i % tiles_per_seq != 0, hp, 0.0)
    hbuf[0, HALO:HALO + tl, :] = a_ref[...].astype(F32) * _sigmoid(g_ref[...].astype(F32))
    _dwconv_rows(hbuf, w_ref, cb_ref, cbuf, tl=tl, taps=taps, post=lambda v: v)

    def ln_body(r, carry):
        r0 = pl.multiple_of(r * CONV_ROWS, CONV_ROWS)
        y = _layer_norm_rows(cbuf[pl.ds(r0, CONV_ROWS), :], lg_ref[...], lb_ref[...])
        o_ref[pl.ds(r0, CONV_ROWS), :] = _silu(y).astype(o_ref.dtype)
        return carry

    lax.fori_loop(0, tl // CONV_ROWS, ln_body, 0)

    ck = ckv_ref[...].astype(F32)
    ckvn_ref[...] = (ck * lax.rsqrt(jnp.mean(ck * ck, axis=-1, keepdims=True) + EPS) * kg_ref[...]).astype(ckvn_ref.dtype)


def conformer_conv(u, col_a, col_g, col_ckv, ch, rank, conv_w, conv_b, ln_g, ln_b, kv_g, seq, tl):
    t = u.shape[0]
    taps = conv_w.shape[0]
    wpad = _tap_weights(conv_w)
    hb = tl // HALO
    prev = lambda col: (lambda i: (jnp.maximum(i * hb - 1, 0), col))
    kern = functools.partial(_conformer_kernel, tl=tl, taps=taps, tiles_per_seq=seq // tl)
    vec = lambda n: pl.BlockSpec((1, n), lambda i: (0, 0))
    return pl.pallas_call(
        kern,
        grid=(t // tl,),
        in_specs=[pl.BlockSpec((tl, ch), lambda i: (i, col_a)),
                  pl.BlockSpec((tl, ch), lambda i: (i, col_g)),
                  pl.BlockSpec((HALO, ch), prev(col_a)),
                  pl.BlockSpec((HALO, ch), prev(col_g)),
                  pl.BlockSpec((tl, rank), lambda i: (i, col_ckv)),
                  pl.BlockSpec((taps, SUBLANES, ch), lambda i: (0, 0, 0)),
                  vec(ch), vec(ch), vec(ch), vec(rank)],
        out_specs=[pl.BlockSpec((tl, ch), lambda i: (i, 0)),
                   pl.BlockSpec((tl, rank), lambda i: (i, 0))],
        out_shape=[jax.ShapeDtypeStruct((t, ch), BF16), jax.ShapeDtypeStruct((t, rank), BF16)],
        scratch_shapes=[pltpu.VMEM((len(_conv_shifts(taps)), HALO + tl, ch), F32), pltpu.VMEM((tl, ch), F32)],
        compiler_params=_cparams("parallel"),
        name="conformer_conv",
    )(u, u, u, u, u, wpad, conv_b.reshape(1, ch), ln_g.reshape(1, ch), ln_b.reshape(1, ch), kv_g.reshape(1, rank))


def _short_conv_kernel(bg_ref, cg_ref, h_ref, cgp_ref, hp_ref, w_ref, o_ref, hbuf, cbuf, *, tl, taps, tiles_per_seq):
    i = pl.program_id(0)
    hp = cgp_ref[...].astype(F32) * hp_ref[...].astype(F32)
    hbuf[0, 0:HALO, :] = jnp.where(i % tiles_per_seq != 0, hp, 0.0)
    hbuf[0, HALO:HALO + tl, :] = cg_ref[...].astype(F32) * h_ref[...].astype(F32)
    _dwconv_rows(hbuf, w_ref, None, cbuf, tl=tl, taps=taps, post=lambda v: v)
    o_ref[...] = (bg_ref[...].astype(F32) * cbuf[...]).astype(o_ref.dtype)


def short_gated_conv(u, col_bg, col_cg, col_h, ch, conv_w, seq, tl):
    t = u.shape[0]
    taps = conv_w.shape[0]
    wpad = _tap_weights(conv_w)
    hb = tl // HALO
    prev = lambda col: (lambda i: (jnp.maximum(i * hb - 1, 0), col))
    kern = functools.partial(_short_conv_kernel, tl=tl, taps=taps, tiles_per_seq=seq // tl)
    return pl.pallas_call(
        kern,
        grid=(t // tl,),
        in_specs=[pl.BlockSpec((tl, ch), lambda i: (i, col_bg)),
                  pl.BlockSpec((tl, ch), lambda i: (i, col_cg)),
                  pl.BlockSpec((tl, ch), lambda i: (i, col_h)),
                  pl.BlockSpec((HALO, ch), prev(col_cg)),
                  pl.BlockSpec((HALO, ch), prev(col_h)),
                  pl.BlockSpec((taps, SUBLANES, ch), lambda i: (0, 0, 0))],
        out_specs=pl.BlockSpec((tl, ch), lambda i: (i, 0)),
        out_shape=jax.ShapeDtypeStruct((t, ch), BF16),
        scratch_shapes=[pltpu.VMEM((len(_conv_shifts(taps)), HALO + tl, ch), F32), pltpu.VMEM((tl, ch), F32)],
        compiler_params=_cparams("parallel"),
        name="short_gated_conv",
    )(u, u, u, u, u, wpad)


QBLK = 128
INT_MIN = -(2 ** 31)
LOG2E = 1.4426950408889634


def _dsa_kernel(q_ref, qia_ref, qib_ref, kwq_ref, ckv_ref, kidx_ref, wuk_ref, wuv_ref, o_ref,
                keys_ref, bias_ref, qlat_ref,
                *, lk, q0, topk, heads, idx_heads, idx_dim, qk_dim, v_dim, tk, hg):
    j = pl.program_id(1)
    qbase = q0 + j * QBLK
    nkc = lk // tk
    contract_last = (((1,), (1,)), ((), ()))

    idx_scale = float(idx_heads * idx_dim) ** -0.5
    widx = kwq_ref[:, idx_dim:idx_dim + idx_heads] * idx_scale
    rowpos = qbase + lax.broadcasted_iota(jnp.int32, (QBLK, tk), 0)
    hpb = qia_ref.shape[1] // idx_dim
    for kc in range(nkc):
        kk = kidx_ref[kc * tk:(kc + 1) * tk, 0:idx_dim].astype(BF16)
        sc = jnp.zeros((QBLK, tk), F32)
        for h in range(idx_heads):
            qi_ref, hl = (qia_ref, h) if h < hpb else (qib_ref, h - hpb)
            lg = lax.dot_general(qi_ref[:, hl * idx_dim:(hl + 1) * idx_dim], kk, contract_last,
                                 preferred_element_type=F32)
            sc = sc + jnp.maximum(lg, 0.0) * widx[:, h:h + 1]
        colpos = kc * tk + lax.broadcasted_iota(jnp.int32, (QBLK, tk), 1)
        bits = lax.bitcast_convert_type(sc, jnp.int32)
        key = jnp.where(bits >= 0, bits, bits ^ jnp.int32(0x7FFFFFFF))
        keys_ref[kc] = jnp.where(colpos <= rowpos, key, jnp.int32(INT_MIN))

    def bit_body(t, tau):
        cand = tau + lax.shift_left(jnp.int32(1), 31 - t)
        cnt = jnp.zeros((QBLK, 1), F32)
        for kc in range(nkc):
            cnt = cnt + jnp.sum(jnp.where(keys_ref[kc] >= cand, 1.0, 0.0), axis=-1, keepdims=True)
        return jnp.where(cnt >= float(topk), cand, tau)

    tau = lax.fori_loop(0, 32, bit_body, jnp.full((QBLK, 1), INT_MIN, jnp.int32))
    tau = jnp.maximum(tau, jnp.int32(INT_MIN + 1))
    for kc in range(nkc):
        bias_ref[:, kc * tk:(kc + 1) * tk] = jnp.where(keys_ref[kc] >= tau, 0.0, NEG_BIG)

    qk_scale = LOG2E * float(qk_dim) ** -0.5
    rank = ckv_ref.shape[1]
    for h in range(heads):
        ql = jnp.dot(q_ref[:, h * qk_dim:(h + 1) * qk_dim], wuk_ref[h], preferred_element_type=F32)
        qlat_ref[h * QBLK:(h + 1) * QBLK, :] = (ql * qk_scale).astype(BF16)

    m_rows = hg * QBLK
    kv = ckv_ref[...]
    bias = bias_ref[...][None]
    for g in range(heads // hg):
        s = lax.dot_general(qlat_ref[g * m_rows:(g + 1) * m_rows, :], kv, contract_last,
                            preferred_element_type=F32)
        s = (s.reshape(hg, QBLK, lk) + bias).reshape(m_rows, lk)
        p = jnp.exp2(s - jnp.max(s, axis=-1, keepdims=True))
        denom = jnp.sum(p, axis=-1, keepdims=True)
        o = (jnp.dot(p.astype(BF16), kv, preferred_element_type=F32) / denom).astype(BF16)
        for r in range(hg):
            h = g * hg + r
            o_ref[:, h * v_dim:(h + 1) * v_dim] = jnp.dot(
                o[r * QBLK:(r + 1) * QBLK, :], wuv_ref[h], preferred_element_type=F32).astype(o_ref.dtype)


def dsa_attention(u3, usm3, ckvn3, wuk, wuv, *, col_q, col_qi, idx_dim, idx_heads, topk, sb_rows, tk, hg):
    bsz, seq, _ = u3.shape
    heads, qk_dim, rank = wuk.shape
    v_dim = wuv.shape[2]
    qi_half = idx_heads * idx_dim // 2
    outs = []
    for sb in range(seq // sb_rows):
        lk = (sb + 1) * sb_rows
        qb0 = sb * (sb_rows // QBLK)
        kern = functools.partial(_dsa_kernel, lk=lk, q0=sb * sb_rows, topk=topk, heads=heads, idx_heads=idx_heads,
                                 idx_dim=idx_dim, qk_dim=qk_dim, v_dim=v_dim, tk=tk, hg=hg)
        outs.append(pl.pallas_call(
            kern,
            grid=(bsz, sb_rows // QBLK),
            in_specs=[pl.BlockSpec((None, QBLK, heads * qk_dim), lambda b, j, qb0=qb0: (b, qb0 + j, col_q)),
                      pl.BlockSpec((None, QBLK, qi_half), lambda b, j, qb0=qb0: (b, qb0 + j, col_qi)),
                      pl.BlockSpec((None, QBLK, qi_half), lambda b, j, qb0=qb0: (b, qb0 + j, col_qi + 1)),
                      pl.BlockSpec((None, QBLK, LANES), lambda b, j, qb0=qb0: (b, qb0 + j, 0)),
                      pl.BlockSpec((None, lk, rank), lambda b, j: (b, 0, 0)),
                      pl.BlockSpec((None, lk, LANES), lambda b, j: (b, 0, 0)),
                      pl.BlockSpec((heads, qk_dim, rank), lambda b, j: (0, 0, 0)),
                      pl.BlockSpec((heads, rank, v_dim), lambda b, j: (0, 0, 0))],
            out_specs=pl.BlockSpec((None, QBLK, heads * v_dim), lambda b, j: (b, j, 0)),
            out_shape=jax.ShapeDtypeStruct((bsz, sb_rows, heads * v_dim), BF16),
            scratch_shapes=[pltpu.VMEM((lk // tk, QBLK, tk), jnp.int32),
                            pltpu.VMEM((QBLK, lk), F32),
                            pltpu.VMEM((heads * QBLK, rank), BF16)],
            compiler_params=_cparams("parallel", "parallel"),
            name=f"dsa_attention_sb{sb}",
        )(u3, u3, u3, usm3, ckvn3, usm3, wuk, wuv))
    return jnp.concatenate(outs, axis=1)


def _softplus(v):
    return jnp.maximum(v, 0.0) + jnp.log1p(jnp.exp(-jnp.abs(v)))


def _split3(v):
    hi = v.astype(BF16)
    r1 = v - hi.astype(F32)
    mid = r1.astype(BF16)
    lo = (r1 - mid.astype(F32)).astype(BF16)
    return hi, mid, lo


def _dot_exact_right(v, m01):
    return sum(jnp.dot(p, m01, preferred_element_type=F32) for p in _split3(v))


def _dot_exact_left(m01, v):
    return sum(jnp.dot(m01, p, preferred_element_type=F32) for p in _split3(v))


def _mamba_kernel(z_ref, xbc_ref, xbcp_ref, dt_ref, dtt_ref, cw_ref, cb_ref, alr_ref, dbr_ref, alc_ref, dbc_ref,
                  dsk_ref, ng_ref, ex_ref, o_ref, hs, xcb, st_ref, ybuf,
                  *, q, taps, heads, hdim, groups, nstate):
    c = pl.program_id(1)
    d_inner = heads * hdim
    gn = groups * nstate
    gw = d_inner // groups
    hpg = heads // groups

    @pl.when(c == 0)
    def _():
        st_ref[...] = jnp.zeros(st_ref.shape, F32)

    hs[0, 0:HALO, :] = jnp.where(c != 0, xbcp_ref[...].astype(F32), 0.0)
    hs[0, HALO:HALO + q, :] = xbc_ref[...].astype(F32)
    _dwconv_rows(hs, cw_ref, cb_ref, xcb, tl=q, taps=taps, post=_silu)

    dt = _softplus(dt_ref[...] + dbr_ref[...])
    a = dt * (-jnp.exp(alr_ref[...]))
    dtt = _softplus(dtt_ref[...] + dbc_ref[...])
    at = dtt * (-jnp.exp(alc_ref[...]))
    ri = lax.broadcasted_iota(jnp.int32, (q, q), 0)
    ci = lax.broadcasted_iota(jnp.int32, (q, q), 1)
    lower = ri >= ci
    a_cum = _dot_exact_left(jnp.where(lower, 1.0, 0.0).astype(BF16), a)
    a_cum_t = _dot_exact_right(at, jnp.where(ri <= ci, 1.0, 0.0).astype(BF16))

    ex = ex_ref[...]
    dt_x = _dot_exact_right(dt, ex)
    acum_x = _dot_exact_right(a_cum, ex)
    alast_x = acum_x[q - 1:q, :]
    xs = xcb[:, 0:d_inner]
    xdt = xs * dt_x
    xdt_b = xdt.astype(BF16)
    xend_b = (xdt * jnp.exp(alast_x - acum_x)).astype(BF16)
    eac_x = jnp.exp(acum_x)
    cdec_x = jnp.exp(alast_x)

    for g in range(groups):
        bg = xcb[:, d_inner + g * nstate:d_inner + (g + 1) * nstate].astype(BF16)
        cg = xcb[:, d_inner + gn + g * nstate:d_inner + gn + (g + 1) * nstate].astype(BF16)
        cbm = lax.dot_general(cg, bg, (((1,), (1,)), ((), ())), preferred_element_type=F32)
        for r in range(hpg):
            h = g * hpg + r
            seg = a_cum[:, h:h + 1] - a_cum_t[h:h + 1, :]
            dec = jnp.exp(jnp.where(lower, seg, NEG_BIG))
            mm = (cbm * dec).astype(BF16)
            ybuf[:, h * hdim:(h + 1) * hdim] = jnp.dot(mm, xdt_b[:, h * hdim:(h + 1) * hdim],
                                                       preferred_element_type=F32)
        lanes = slice(g * gw, (g + 1) * gw)
        st = st_ref[g]
        ybuf[:, lanes] = ybuf[:, lanes] + jnp.dot(cg, st.astype(BF16), preferred_element_type=F32) * eac_x[:, lanes]
        new = lax.dot_general(bg, xend_b[:, lanes], (((0,), (0,)), ((), ())), preferred_element_type=F32)
        st_ref[g] = st * cdec_x[:, lanes] + new

    y = ybuf[...] + xs * dsk_ref[...]
    y = y * _silu(z_ref[...].astype(F32))
    y = y * lax.rsqrt(jnp.mean(y * y, axis=-1, keepdims=True) + EPS) * ng_ref[...]
    o_ref[...] = y.astype(o_ref.dtype)


def mamba2_mixer(u, dts, dtst, col_z, col_xbc, conv_w, conv_b, a_log, dt_bias, d_skip, norm_g,
                 *, bsz, seq, q, heads, hdim, groups, nstate):
    t = u.shape[0]
    d_inner = heads * hdim
    xw = d_inner + 2 * groups * nstate
    taps = conv_w.shape[0]
    nc = seq // q
    hb = q // HALO
    wpad = _tap_weights(conv_w)
    row = lambda v: jnp.zeros((1, LANES), F32).at[0, :heads].set(v)
    col = lambda v: v.reshape(heads, 1).astype(F32)
    expand = (jnp.arange(d_inner)[None, :] // hdim == jnp.arange(LANES)[:, None]).astype(BF16)
    kern = functools.partial(_mamba_kernel, q=q, taps=taps, heads=heads, hdim=hdim, groups=groups, nstate=nstate)
    rowi = lambda b, c: b * nc + c
    full = lambda shp: pl.BlockSpec(shp, lambda b, c: (0,) * len(shp))
    return pl.pallas_call(
        kern,
        grid=(bsz, nc),
        in_specs=[pl.BlockSpec((q, d_inner), lambda b, c: (rowi(b, c), col_z)),
                  pl.BlockSpec((q, xw), lambda b, c: (rowi(b, c), col_xbc)),
                  pl.BlockSpec((HALO, xw), lambda b, c: (jnp.maximum(rowi(b, c) * hb - 1, 0), col_xbc)),
                  pl.BlockSpec((q, LANES), lambda b, c: (rowi(b, c), 0)),
                  pl.BlockSpec((heads, q), lambda b, c: (0, rowi(b, c))),
                  full((taps, SUBLANES, xw)), full((1, xw)),
                  full((1, LANES)), full((1, LANES)), full((heads, 1)), full((heads, 1)),
                  full((1, d_inner)), full((1, d_inner)), full((LANES, d_inner))],
        out_specs=pl.BlockSpec((q, d_inner), lambda b, c: (rowi(b, c), 0)),
        out_shape=jax.ShapeDtypeStruct((t, d_inner), BF16),
        scratch_shapes=[pltpu.VMEM((len(_conv_shifts(taps)), HALO + q, xw), F32),
                        pltpu.VMEM((q, xw), F32),
                        pltpu.VMEM((groups, nstate, d_inner // groups), F32),
                        pltpu.VMEM((q, d_inner), F32)],
        compiler_params=_cparams("parallel", "arbitrary"),
        name="mamba2_ssd",
    )(u, u, u, dts, dtst, wpad, conv_b.reshape(1, xw), row(a_log), row(dt_bias), col(a_log), col(dt_bias),
      jnp.repeat(d_skip.astype(F32), hdim).reshape(1, d_inner), norm_g.reshape(1, d_inner), expand)


ROUTER_ROWS = 8


def _router_kernel(x_ref, rwt_ref, rb_ref, o_ref, *, groups, epg, top_k):
    lg = lax.dot_general(rwt_ref[...], x_ref[...], (((1,), (1,)), ((), ())), preferred_element_type=F32)
    sc = _sigmoid(lg)
    sel = sc + rb_ref[...]
    s = [sel[p * groups:(p + 1) * groups, :] for p in range(epg)]
    w = [sc[p * groups:(p + 1) * groups, :] for p in range(epg)]
    gs = None
    for p in range(epg):
        for r in range(p + 1, epg):
            pair = s[p] + s[r]
            gs = pair if gs is None else jnp.maximum(gs, pair)
    gidx = lax.broadcasted_iota(jnp.int32, gs.shape, 0)
    gmax = jnp.max(gs, axis=0, keepdims=True)
    gbest = jnp.min(jnp.where(gs == gmax, gidx, groups), axis=0, keepdims=True)
    pick = gidx == gbest
    v = [jnp.sum(jnp.where(pick, s[p], 0.0), axis=0, keepdims=True) for p in range(epg)]
    wv = [jnp.sum(jnp.where(pick, w[p], 0.0), axis=0, keepdims=True) for p in range(epg)]
    chosen = []
    for p in range(epg):
        rank = jnp.zeros_like(v[p])
        for r in range(epg):
            if r == p:
                continue
            ahead = (v[r] >= v[p]) if r < p else (v[r] > v[p])
            rank = rank + jnp.where(ahead, 1.0, 0.0)
        chosen.append(jnp.where(rank < float(top_k), wv[p], 0.0))
    tot = chosen[0]
    for p in range(1, epg):
        tot = tot + chosen[p]
    o_ref[...] = jnp.zeros(o_ref.shape, F32)
    for p in range(epg):
        o_ref[p:p + 1, :] = chosen[p] / tot
    o_ref[epg:epg + 1, :] = gbest.astype(F32)


def moe_router(xb, router_w, router_bias, *, groups, epg, top_k, tm):
    t, d = xb.shape
    e = groups * epg
    order = jnp.arange(e).reshape(groups, epg).T.reshape(-1)
    rwt = router_w.T[order].astype(BF16)
    rb = router_bias[order].reshape(e, 1).astype(F32)
    kern = functools.partial(_router_kernel, groups=groups, epg=epg, top_k=top_k)
    return pl.pallas_call(
        kern,
        grid=(t // tm,),
        in_specs=[pl.BlockSpec((tm, d), lambda i: (i, 0)),
                  pl.BlockSpec((e, d), lambda i: (0, 0)),
                  pl.BlockSpec((e, 1), lambda i: (0, 0))],
        out_specs=pl.BlockSpec((ROUTER_ROWS, tm), lambda i: (0, i)),
        out_shape=jax.ShapeDtypeStruct((ROUTER_ROWS, t), F32),
        compiler_params=_cparams("parallel"),
        name="moe_router",
    )(xb, rwt, rb)


PREP_ROWS = 1024


def _moe_prep_up_kernel(g_ref, u_ref, go_ref, uo_ref):
    epg, _, f = g_ref.shape
    for p in range(epg):
        go_ref[:, p * f:(p + 1) * f] = g_ref[p].astype(BF16)
        uo_ref[:, p * f:(p + 1) * f] = u_ref[p].astype(BF16)


def _moe_prep_down_kernel(d_ref, do_ref):
    do_ref[...] = d_ref[...].astype(BF16)


def moe_weight_prep(w_gate, w_up, w_down, layer, groups, epg):
    _, e, d, f = w_gate.shape
    rows = min(PREP_ROWS, d)
    up_in = pl.BlockSpec((None, epg, rows, f), lambda g, c: (layer, g, c, 0))
    up_out = pl.BlockSpec((None, rows, epg * f), lambda g, c: (g, c, 0))
    wg, wu = pl.pallas_call(
        _moe_prep_up_kernel,
        grid=(groups, d // rows),
        in_specs=[up_in, up_in],
        out_specs=[up_out, up_out],
        out_shape=[jax.ShapeDtypeStruct((groups, d, epg * f), BF16)] * 2,
        compiler_params=_cparams("parallel", "parallel"),
        name="moe_weight_prep_up",
    )(w_gate, w_up)
    wd = pl.pallas_call(
        _moe_prep_down_kernel,
        grid=(e,),
        in_specs=[pl.BlockSpec((None, None, f, d), lambda ei: (layer, ei, 0, 0))],
        out_specs=pl.BlockSpec((None, f, d), lambda ei: (ei // epg, ei % epg, 0)),
        out_shape=jax.ShapeDtypeStruct((groups, epg * f, d), BF16),
        compiler_params=_cparams("parallel"),
        name="moe_weight_prep_down",
    )(w_down)
    return wg, wu, wd


def _moe_up_kernel(tg_ref, x_ref, gate_ref, wg_ref, wu_ref, h_ref, *, epg, dff):
    x = x_ref[...]
    g = jnp.dot(x, wg_ref[...], preferred_element_type=F32)
    u = jnp.dot(x, wu_ref[...], preferred_element_type=F32)
    h = _silu(g) * u
    gates = gate_ref[...]
    for p in range(epg):
        h_ref[:, p * dff:(p + 1) * dff] = (h[:, p * dff:(p + 1) * dff] * gates[:, p:p + 1]).astype(h_ref.dtype)


def _moe_down_kernel(tg_ref, h_ref, wd_ref, x_ref, g_ref, b_ref, o_ref, *, alpha):
    ffn = jnp.dot(h_ref[...], wd_ref[...], preferred_element_type=F32)
    o_ref[...] = _layer_norm_rows(alpha * x_ref[...] + ffn, g_ref[...], b_ref[...])


def moe_experts(xs_b, xs_f, gates_s, tile_group, wg, wu, wd, ln_g, ln_b, *, alpha, epg, tm):
    p_rows, d = xs_b.shape
    f4 = wg.shape[2]
    dff = f4 // epg
    n_tiles = p_rows // tm
    once = pl.Buffered(1)
    h = pl.pallas_call(
        functools.partial(_moe_up_kernel, epg=epg, dff=dff),
        grid_spec=pltpu.PrefetchScalarGridSpec(
            num_scalar_prefetch=1,
            grid=(n_tiles,),
            in_specs=[pl.BlockSpec((tm, d), lambda i, tg: (i, 0)),
                      pl.BlockSpec((tm, epg), lambda i, tg: (i, 0)),
                      pl.BlockSpec((None, d, f4), lambda i, tg: (tg[i], 0, 0), pipeline_mode=once),
                      pl.BlockSpec((None, d, f4), lambda i, tg: (tg[i], 0, 0), pipeline_mode=once)],
            out_specs=pl.BlockSpec((tm, f4), lambda i, tg: (i, 0))),
        out_shape=jax.ShapeDtypeStruct((p_rows, f4), BF16),
        compiler_params=_cparams("arbitrary"),
        name="moe_gate_up",
    )(tile_group, xs_b, gates_s, wg, wu)
    return pl.pallas_call(
        functools.partial(_moe_down_kernel, alpha=alpha),
        grid_spec=pltpu.PrefetchScalarGridSpec(
            num_scalar_prefetch=1,
            grid=(n_tiles,),
            in_specs=[pl.BlockSpec((tm, f4), lambda i, tg: (i, 0)),
                      pl.BlockSpec((None, f4, d), lambda i, tg: (tg[i], 0, 0), pipeline_mode=once),
                      pl.BlockSpec((tm, d), lambda i, tg: (i, 0)),
                      pl.BlockSpec((1, d), lambda i, tg: (0, 0)),
                      pl.BlockSpec((1, d), lambda i, tg: (0, 0))],
            out_specs=pl.BlockSpec((tm, d), lambda i, tg: (i, 0))),
        out_shape=jax.ShapeDtypeStruct((p_rows, d), F32),
        compiler_params=_cparams("arbitrary"),
        name="moe_down_ln",
    )(tile_group, h, wd, xs_f, ln_g, ln_b)


def moe_block(x, xb, router_w, router_bias, wg, wu, wd, ln_g, ln_b, *, alpha, groups, epg, top_k, tm):
    t, d = x.shape
    rt = moe_router(xb, router_w, router_bias, groups=groups, epg=epg, top_k=top_k, tm=512)
    gbest = rt[epg].astype(jnp.int32)
    gates = rt[:epg].T
    onehot = (gbest[:, None] == jnp.arange(groups)[None, :]).astype(jnp.int32)
    counts = jnp.sum(onehot, axis=0)
    padded = ((counts + tm - 1) // tm) * tm
    ends = jnp.cumsum(padded)
    rank = jnp.take_along_axis(jnp.cumsum(onehot, axis=0) - onehot, gbest[:, None], axis=1)[:, 0]
    pos = (ends - padded)[gbest] + rank
    p_rows = t + groups * tm
    src = jnp.zeros((p_rows,), jnp.int32).at[pos].set(jnp.arange(t, dtype=jnp.int32))
    gates_s = jnp.zeros((p_rows, epg), F32).at[pos].set(gates)
    tile_start = jnp.arange(p_rows // tm, dtype=jnp.int32) * tm
    tile_group = jnp.minimum(jnp.sum(tile_start[:, None] >= ends[None, :], axis=1), groups - 1).astype(jnp.int32)
    xs_b = xb[src]
    xs_f = x[src]
    out_s = moe_experts(xs_b, xs_f, gates_s, tile_group, wg, wu, wd, ln_g, ln_b, alpha=alpha, epg=epg, tm=tm)
    x2 = out_s[pos]
    return x2, x2.astype(BF16)


def kernel(x, even_w_in, even_conv_a_w, even_conv_a_b, even_ln_a_g, even_ln_a_b, even_kv_norm_g, even_w_uk, even_w_uv, even_w_out, odd_w_in, odd_conv_c_w, odd_conv_d_w, odd_conv_d_b, odd_a_log, odd_dt_bias, odd_d_skip, odd_norm_g, odd_w_out, ln1_g, ln1_b, ln2_g, ln2_b, router_w, router_bias, moe_w_gate, moe_w_up, moe_w_down):
    bsz, seq, d = x.shape
    t = bsz * seq
    depth = ln1_g.shape[0]
    alpha = (2 * depth) ** 0.25

    a_ch = even_conv_a_w.shape[2]
    heads, qk_dim, rank = even_w_uk.shape[1:]
    v_dim = even_w_uv.shape[3]
    idx_dim, idx_heads = 64, 16
    q_w = heads * qk_dim
    qi_w = idx_heads * idx_dim
    topk = min(256, seq // 4)

    c_ch = odd_conv_c_w.shape[2]
    d_heads = odd_a_log.shape[1]
    d_inner = odd_norm_g.shape[1]
    xw = odd_conv_d_w.shape[2]
    d_groups, d_state, d_chunk = 8, 128, 128

    n_experts, _, dff = moe_w_gate.shape[1:]
    groups, top_k = 8, 2
    epg = n_experts // groups

    xf = x.reshape(t, d)
    xb = xf.astype(BF16)

    def pad_rows(w):
        return jnp.pad(w, ((0, LANES - w.shape[0]), (0, 0)))

    even_wt = jnp.swapaxes(even_w_in, 1, 2)
    odd_wt = jnp.swapaxes(odd_w_in, 1, 2)
    proj_tm, proj_tn = 512, 1536
    for layer in range(depth):
        i = layer // 2
        if layer % 2 == 0:
            o_q = 2 * a_ch
            o_ckv = o_q + q_w
            o_qi = o_ckv + rank
            o_k = o_qi + qi_w
            u = matmul(xb, even_wt, i, o_k, BF16, proj_tm, proj_tn)
            usm, _ = matmul_small(xb, pad_rows(even_wt[i, o_k:, :]), 512)
            y_a, ckvn = conformer_conv(u, 0, 1, o_ckv // rank, a_ch, rank,
                                       even_conv_a_w[i], even_conv_a_b[i], even_ln_a_g[i], even_ln_a_b[i],
                                       even_kv_norm_g[i], seq, 256)
            y_b = dsa_attention(u.reshape(bsz, seq, -1), usm.reshape(bsz, seq, LANES), ckvn.reshape(bsz, seq, rank),
                                even_w_uk[i].astype(BF16), even_w_uv[i].astype(BF16),
                                col_q=o_q // q_w, col_qi=o_qi // (qi_w // 2), idx_dim=idx_dim, idx_heads=idx_heads,
                                topk=topk, sb_rows=512, tk=512, hg=4).reshape(t, heads * v_dim)
            w_out = even_w_out[i].astype(BF16)
            y1, y2 = y_a, y_b
        else:
            o_dt = 3 * c_ch + d_inner + xw
            u = matmul(xb, odd_wt, i, o_dt, BF16, proj_tm, proj_tn)
            dts, dtst = matmul_small(xb, pad_rows(odd_wt[i, o_dt:, :]), 512)
            y_c = short_gated_conv(u, 0, 1, 2, c_ch, odd_conv_c_w[i], seq, 256)
            y_d = mamba2_mixer(u, dts, dtst, 3 * c_ch // d_inner, (3 * c_ch + d_inner) // xw,
                               odd_conv_d_w[i], odd_conv_d_b[i], odd_a_log[i], odd_dt_bias[i], odd_d_skip[i],
                               odd_norm_g[i], bsz=bsz, seq=seq, q=d_chunk, heads=d_heads, hdim=d_inner // d_heads,
                               groups=d_groups, nstate=d_state)
            w_out = odd_w_out[i].astype(BF16)
            y1, y2 = y_c, y_d
        xf, xb = matmul_residual_ln(y1, y2, w_out, xf, ln1_g[layer].reshape(1, d), ln1_b[layer].reshape(1, d),
                                    alpha, 512, 512)
        wg, wu, wd = moe_weight_prep(moe_w_gate, moe_w_up, moe_w_down, layer, groups, epg)
        xf, xb = moe_block(xf, xb, router_w, router_bias, wg, wu, wd,
                           ln2_g[layer].reshape(1, d), ln2_b[layer].reshape(1, d),
                           alpha=alpha, groups=groups, epg=epg, top_k=top_k, tm=256)
    return xf.reshape(bsz, seq, d)
```

```python
import functools
import math

import jax
import jax.numpy as jnp
from jax import lax
from jax.experimental import pallas as pl
from jax.experimental.pallas import tpu as pltpu

F32 = jnp.float32
BF16 = jnp.bfloat16

VMEM_LIMIT_BYTES = 56 * 1024 * 1024
LANES = 128
HALO = 32
NEG_BIG = -1e30

EPS = 1e-5


def _cparams(*sem):
    return pltpu.CompilerParams(dimension_semantics=sem, vmem_limit_bytes=VMEM_LIMIT_BYTES)


def _sigmoid(v):
    return 1.0 / (1.0 + jnp.exp(-v))


def _silu(v):
    return v * _sigmoid(v)


CONTRACT_LAST = (((1,), (1,)), ((), ()))


def _mm_kernel(x_ref, wt_ref, o_ref, wb_ref):
    @pl.when(pl.program_id(1) == 0)
    def _():
        wb_ref[...] = wt_ref[...].astype(BF16)

    o_ref[...] = lax.dot_general(x_ref[...], wb_ref[...], CONTRACT_LAST,
                                 preferred_element_type=F32).astype(o_ref.dtype)


def matmul(x, wt, layer, n, out_dtype, tm, tn):
    m, k = x.shape
    return pl.pallas_call(
        _mm_kernel,
        grid=(n // tn, m // tm),
        in_specs=[pl.BlockSpec((tm, k), lambda j, i: (i, 0)),
                  pl.BlockSpec((None, tn, k), lambda j, i: (layer, j, 0), pipeline_mode=pl.Buffered(1))],
        out_specs=pl.BlockSpec((tm, tn), lambda j, i: (i, j)),
        out_shape=jax.ShapeDtypeStruct((m, n), out_dtype),
        scratch_shapes=[pltpu.VMEM((tn, k), BF16)],
        compiler_params=_cparams("arbitrary", "arbitrary"),
        name="proj_matmul",
    )(x, wt)


def _mm_small_kernel(x_ref, wt_ref, o_ref, ot_ref):
    xb = x_ref[...]
    wb = wt_ref[...].astype(BF16)
    o_ref[...] = lax.dot_general(xb, wb, CONTRACT_LAST, preferred_element_type=F32)
    ot_ref[...] = lax.dot_general(wb, xb, CONTRACT_LAST, preferred_element_type=F32)


def matmul_small(x, wt, tm):
    m, k = x.shape
    n = wt.shape[0]
    return pl.pallas_call(
        _mm_small_kernel,
        grid=(m // tm,),
        in_specs=[pl.BlockSpec((tm, k), lambda i: (i, 0)),
                  pl.BlockSpec((n, k), lambda i: (0, 0))],
        out_specs=[pl.BlockSpec((tm, n), lambda i: (i, 0)),
                   pl.BlockSpec((n, tm), lambda i: (0, i))],
        out_shape=[jax.ShapeDtypeStruct((m, n), F32), jax.ShapeDtypeStruct((n, m), F32)],
        compiler_params=_cparams("parallel"),
        name="proj_small",
    )(x, wt)


def _layer_norm_rows(v, g, b):
    mu = jnp.mean(v, axis=-1, keepdims=True)
    d = v - mu
    var = jnp.mean(d * d, axis=-1, keepdims=True)
    return d * lax.rsqrt(var + EPS) * g + b


LN_ROWS = 128


def _mm_ln_kernel(y1_ref, y2_ref, w_ref, x_ref, g_ref, b_ref, o_ref, ob_ref, *, nj, tn, alpha):
    j = pl.program_id(1)
    y = jnp.concatenate([y1_ref[...], y2_ref[...]], axis=1)
    pre = jnp.dot(y, w_ref[...], preferred_element_type=F32) + alpha * x_ref[...]
    o_ref[:, pl.ds(pl.multiple_of(j * tn, tn), tn)] = pre

    @pl.when(j == nj - 1)
    def _():
        def rows(r, carry):
            r0 = pl.multiple_of(r * LN_ROWS, LN_ROWS)
            v = _layer_norm_rows(o_ref[pl.ds(r0, LN_ROWS), :], g_ref[...], b_ref[...])
            o_ref[pl.ds(r0, LN_ROWS), :] = v
            ob_ref[pl.ds(r0, LN_ROWS), :] = v.astype(BF16)
            return carry

        lax.fori_loop(0, o_ref.shape[0] // LN_ROWS, rows, 0)


def matmul_residual_ln(y1, y2, w, x, g, b, alpha, tm, tn):
    m, k1 = y1.shape
    k2 = y2.shape[1]
    n = w.shape[1]
    nj = n // tn
    kern = functools.partial(_mm_ln_kernel, nj=nj, tn=tn, alpha=alpha)
    once = pl.Buffered(1)
    return pl.pallas_call(
        kern,
        grid=(m // tm, nj),
        in_specs=[pl.BlockSpec((tm, k1), lambda i, j: (i, 0), pipeline_mode=once),
                  pl.BlockSpec((tm, k2), lambda i, j: (i, 0), pipeline_mode=once),
                  pl.BlockSpec((k1 + k2, tn), lambda i, j: (0, j)),
                  pl.BlockSpec((tm, tn), lambda i, j: (i, j)),
                  pl.BlockSpec((1, n), lambda i, j: (0, 0)),
                  pl.BlockSpec((1, n), lambda i, j: (0, 0))],
        out_specs=[pl.BlockSpec((tm, n), lambda i, j: (i, 0), pipeline_mode=once),
                   pl.BlockSpec((tm, n), lambda i, j: (i, 0), pipeline_mode=once)],
        out_shape=[jax.ShapeDtypeStruct((m, n), F32), jax.ShapeDtypeStruct((m, n), BF16)],
        compiler_params=_cparams("parallel", "arbitrary"),
        name="outproj_ln",
    )(y1, y2, w, x, g, b)


CONV_ROWS = 32
CONV_LANES = 512


SUBLANES = 8


def _conv_shifts(taps):
    off = HALO - (taps - 1)
    return sorted({(off + k) % SUBLANES for k in range(taps)} | {0})


def _tap_weights(conv_w):
    taps, c = conv_w.shape
    return jnp.broadcast_to(conv_w.astype(F32)[:, None, :], (taps, SUBLANES, c))


def _dwconv_rows(hs, w_ref, b_ref, dst, *, tl, taps, post):
    c = dst.shape[1]
    off = HALO - (taps - 1)
    shifts = _conv_shifts(taps)
    nr = tl + HALO - SUBLANES
    for si, s in enumerate(shifts):
        if s:
            hs[si, 0:nr, :] = hs[0, s:s + nr, :]

    def row_body(r, carry):
        r0 = pl.multiple_of(r * CONV_ROWS, CONV_ROWS)
        for cj in range(c // CONV_LANES):
            cs = slice(cj * CONV_LANES, (cj + 1) * CONV_LANES)
            if b_ref is None:
                acc = jnp.zeros((CONV_ROWS, CONV_LANES), F32)
            else:
                acc = jnp.broadcast_to(b_ref[:, cs], (CONV_ROWS, CONV_LANES))
            for k in range(taps):
                o = off + k
                si = shifts.index(o % SUBLANES)
                wk = jnp.concatenate([w_ref[k, :, cs]] * (CONV_ROWS // SUBLANES), axis=0)
                acc = acc + hs[si, pl.ds(r0 + (o - o % SUBLANES), CONV_ROWS), cs] * wk
            dst[pl.ds(r0, CONV_ROWS), cs] = post(acc).astype(dst.dtype)
        return carry

    lax.fori_loop(0, tl // CONV_ROWS, row_body, 0)


def _conformer_kernel(a_ref, g_ref, ap_ref, gp_ref, ckv_ref, w_ref, cb_ref, lg_ref, lb_ref, kg_ref,
                      o_ref, ckvn_ref, hbuf, cbuf, *, tl, taps, tiles_per_seq):
    i = pl.program_id(0)
    hp = ap_ref[...].astype(F32) * _sigmoid(gp_ref[...].astype(F32))
    hbuf[0, 0:HALO, :] = jnp.where(i % tiles_per_seq != 0, hp, 0.0)
    hbuf[0, HALO:HALO + tl, :] = a_ref[...].astype(F32) * _sigmoid(g_ref[...].astype(F32))
    _dwconv_rows(hbuf, w_ref, cb_ref, cbuf, tl=tl, taps=taps, post=lambda v: v)

    def ln_body(r, carry):
        r0 = pl.multiple_of(r * CONV_ROWS, CONV_ROWS)
        y = _layer_norm_rows(cbuf[pl.ds(r0, CONV_ROWS), :], lg_ref[...], lb_ref[...])
        o_ref[pl.ds(r0, CONV_ROWS), :] = _silu(y).astype(o_ref.dtype)
        return carry

    lax.fori_loop(0, tl // CONV_ROWS, ln_body, 0)

    ck = ckv_ref[...].astype(F32)
    ckvn_ref[...] = (ck * lax.rsqrt(jnp.mean(ck * ck, axis=-1, keepdims=True) + EPS) * kg_ref[...]).astype(ckvn_ref.dtype)


def conformer_conv(u, col_a, col_g, col_ckv, ch, rank, conv_w, conv_b, ln_g, ln_b, kv_g, seq, tl):
    t = u.shape[0]
    taps = conv_w.shape[0]
    wpad = _tap_weights(conv_w)
    hb = tl // HALO
    prev = lambda col: (lambda i: (jnp.maximum(i * hb - 1, 0), col))
    kern = functools.partial(_conformer_kernel, tl=tl, taps=taps, tiles_per_seq=seq // tl)
    vec = lambda n: pl.BlockSpec((1, n), lambda i: (0, 0))
    return pl.pallas_call(
        kern,
        grid=(t // tl,),
        in_specs=[pl.BlockSpec((tl, ch), lambda i: (i, col_a)),
                  pl.BlockSpec((tl, ch), lambda i: (i, col_g)),
                  pl.BlockSpec((HALO, ch), prev(col_a)),
                  pl.BlockSpec((HALO, ch), prev(col_g)),
                  pl.BlockSpec((tl, rank), lambda i: (i, col_ckv)),
                  pl.BlockSpec((taps, SUBLANES, ch), lambda i: (0, 0, 0)),
                  vec(ch), vec(ch), vec(ch), vec(rank)],
        out_specs=[pl.BlockSpec((tl, ch), lambda i: (i, 0)),
                   pl.BlockSpec((tl, rank), lambda i: (i, 0))],
        out_shape=[jax.ShapeDtypeStruct((t, ch), BF16), jax.ShapeDtypeStruct((t, rank), BF16)],
        scratch_shapes=[pltpu.VMEM((len(_conv_shifts(taps)), HALO + tl, ch), F32), pltpu.VMEM((tl, ch), F32)],
        compiler_params=_cparams("parallel"),
        name="conformer_conv",
    )(u, u, u, u, u, wpad, conv_b.reshape(1, ch), ln_g.reshape(1, ch), ln_b.reshape(1, ch), kv_g.reshape(1, rank))


def _short_conv_kernel(bg_ref, cg_ref, h_ref, cgp_ref, hp_ref, w_ref, o_ref, hbuf, cbuf, *, tl, taps, tiles_per_seq):
    i = pl.program_id(0)
    hp = cgp_ref[...].astype(F32) * hp_ref[...].astype(F32)
    hbuf[0, 0:HALO, :] = jnp.where(i % tiles_per_seq != 0, hp, 0.0)
    hbuf[0, HALO:HALO + tl, :] = cg_ref[...].astype(F32) * h_ref[...].astype(F32)
    _dwconv_rows(hbuf, w_ref, None, cbuf, tl=tl, taps=taps, post=lambda v: v)
    o_ref[...] = (bg_ref[...].astype(F32) * cbuf[...]).astype(o_ref.dtype)


def short_gated_conv(u, col_bg, col_cg, col_h, ch, conv_w, seq, tl):
    t = u.shape[0]
    taps = conv_w.shape[0]
    wpad = _tap_weights(conv_w)
    hb = tl // HALO
    prev = lambda col: (lambda i: (jnp.maximum(i * hb - 1, 0), col))
    kern = functools.partial(_short_conv_kernel, tl=tl, taps=taps, tiles_per_seq=seq // tl)
    return pl.pallas_call(
        kern,
        grid=(t // tl,),
        in_specs=[pl.BlockSpec((tl, ch), lambda i: (i, col_bg)),
                  pl.BlockSpec((tl, ch), lambda i: (i, col_cg)),
                  pl.BlockSpec((tl, ch), lambda i: (i, col_h)),
                  pl.BlockSpec((HALO, ch), prev(col_cg)),
                  pl.BlockSpec((HALO, ch), prev(col_h)),
                  pl.BlockSpec((taps, SUBLANES, ch), lambda i: (0, 0, 0))],
        out_specs=pl.BlockSpec((tl, ch), lambda i: (i, 0)),
        out_shape=jax.ShapeDtypeStruct((t, ch), BF16),
        scratch_shapes=[pltpu.VMEM((len(_conv_shifts(taps)), HALO + tl, ch), F32), pltpu.VMEM((tl, ch), F32)],
        compiler_params=_cparams("parallel"),
        name="short_gated_conv",
    )(u, u, u, u, u, wpad)


QBLK = 128
INT_MIN = -(2 ** 31)
LOG2E = 1.4426950408889634


def _dsa_kernel(q_ref, qia_ref, qib_ref, kwq_ref, ckv_ref, kidx_ref, wuk_ref, wuv_ref, o_ref,
                keys_ref, bias_ref, qlat_ref,
                *, lk, q0, topk, heads, idx_heads, idx_dim, qk_dim, v_dim, tk, hg):
    j = pl.program_id(1)
    qbase = q0 + j * QBLK
    nkc = lk // tk
    contract_last = (((1,), (1,)), ((), ()))

    idx_scale = float(idx_heads * idx_dim) ** -0.5
    widx = kwq_ref[:, idx_dim:idx_dim + idx_heads] * idx_scale
    rowpos = qbase + lax.broadcasted_iota(jnp.int32, (QBLK, tk), 0)
    hpb = qia_ref.shape[1] // idx_dim
    for kc in range(nkc):
        kk = kidx_ref[kc * tk:(kc + 1) * tk, 0:idx_dim].astype(BF16)
        sc = jnp.zeros((QBLK, tk), F32)
        for h in range(idx_heads):
            qi_ref, hl = (qia_ref, h) if h < hpb else (qib_ref, h - hpb)
            lg = lax.dot_general(qi_ref[:, hl * idx_dim:(hl + 1) * idx_dim], kk, contract_last,
                                 preferred_element_type=F32)
            sc = sc + jnp.maximum(lg, 0.0) * widx[:, h:h + 1]
        colpos = kc * tk + lax.broadcasted_iota(jnp.int32, (QBLK, tk), 1)
        bits = lax.bitcast_convert_type(sc, jnp.int32)
        key = jnp.where(bits >= 0, bits, bits ^ jnp.int32(0x7FFFFFFF))
        keys_ref[kc] = jnp.where(colpos <= rowpos, key, jnp.int32(INT_MIN))

    def bit_body(t, tau):
        cand = tau + lax.shift_left(jnp.int32(1), 31 - t)
        cnt = jnp.zeros((QBLK, 1), F32)
        for kc in range(nkc):
            cnt = cnt + jnp.sum(jnp.where(keys_ref[kc] >= cand, 1.0, 0.0), axis=-1, keepdims=True)
        return jnp.where(cnt >= float(topk), cand, tau)

    tau = lax.fori_loop(0, 32, bit_body, jnp.full((QBLK, 1), INT_MIN, jnp.int32))
    tau = jnp.maximum(tau, jnp.int32(INT_MIN + 1))
    for kc in range(nkc):
        bias_ref[:, kc * tk:(kc + 1) * tk] = jnp.where(keys_ref[kc] >= tau, 0.0, NEG_BIG)

    qk_scale = LOG2E * float(qk_dim) ** -0.5
    rank = ckv_ref.shape[1]
    for h in range(heads):
        ql = jnp.dot(q_ref[:, h * qk_dim:(h + 1) * qk_dim], wuk_ref[h], preferred_element_type=F32)
        qlat_ref[h * QBLK:(h + 1) * QBLK, :] = (ql * qk_scale).astype(BF16)

    m_rows = hg * QBLK
    kv = ckv_ref[...]
    bias = bias_ref[...][None]
    for g in range(heads // hg):
        s = lax.dot_general(qlat_ref[g * m_rows:(g + 1) * m_rows, :], kv, contract_last,
                            preferred_element_type=F32)
        s = (s.reshape(hg, QBLK, lk) + bias).reshape(m_rows, lk)
        p = jnp.exp2(s - jnp.max(s, axis=-1, keepdims=True))
        denom = jnp.sum(p, axis=-1, keepdims=True)
        o = (jnp.dot(p.astype(BF16), kv, preferred_element_type=F32) / denom).astype(BF16)
        for r in range(hg):
            h = g * hg + r
            o_ref[:, h * v_dim:(h + 1) * v_dim] = jnp.dot(
                o[r * QBLK:(r + 1) * QBLK, :], wuv_ref[h], preferred_element_type=F32).astype(o_ref.dtype)


def dsa_attention(u3, usm3, ckvn3, wuk, wuv, *, col_q, col_qi, idx_dim, idx_heads, topk, sb_rows, tk, hg):
    bsz, seq, _ = u3.shape
    heads, qk_dim, rank = wuk.shape
    v_dim = wuv.shape[2]
    qi_half = idx_heads * idx_dim // 2
    outs = []
    for sb in range(seq // sb_rows):
        lk = (sb + 1) * sb_rows
        qb0 = sb * (sb_rows // QBLK)
        kern = functools.partial(_dsa_kernel, lk=lk, q0=sb * sb_rows, topk=topk, heads=heads, idx_heads=idx_heads,
                                 idx_dim=idx_dim, qk_dim=qk_dim, v_dim=v_dim, tk=tk, hg=hg)
        outs.append(pl.pallas_call(
            kern,
            grid=(bsz, sb_rows // QBLK),
            in_specs=[pl.BlockSpec((None, QBLK, heads * qk_dim), lambda b, j, qb0=qb0: (b, qb0 + j, col_q)),
                      pl.BlockSpec((None, QBLK, qi_half), lambda b, j, qb0=qb0: (b, qb0 + j, col_qi)),
                      pl.BlockSpec((None, QBLK, qi_half), lambda b, j, qb0=qb0: (b, qb0 + j, col_qi + 1)),
                      pl.BlockSpec((None, QBLK, LANES), lambda b, j, qb0=qb0: (b, qb0 + j, 0)),
                      pl.BlockSpec((None, lk, rank), lambda b, j: (b, 0, 0)),
                      pl.BlockSpec((None, lk, LANES), lambda b, j: (b, 0, 0)),
                      pl.BlockSpec((heads, qk_dim, rank), lambda b, j: (0, 0, 0)),
                      pl.BlockSpec((heads, rank, v_dim), lambda b, j: (0, 0, 0))],
            out_specs=pl.BlockSpec((None, QBLK, heads * v_dim), lambda b, j: (b, j, 0)),
            out_shape=jax.ShapeDtypeStruct((bsz, sb_rows, heads * v_dim), BF16),
            scratch_shapes=[pltpu.VMEM((lk // tk, QBLK, tk), jnp.int32),
                            pltpu.VMEM((QBLK, lk), F32),
                            pltpu.VMEM((heads * QBLK, rank), BF16)],
            compiler_params=_cparams("parallel", "parallel"),
            name=f"dsa_attention_sb{sb}",
        )(u3, u3, u3, usm3, ckvn3, usm3, wuk, wuv))
    return jnp.concatenate(outs, axis=1)


def _softplus(v):
    return jnp.maximum(v, 0.0) + jnp.log1p(jnp.exp(-jnp.abs(v)))


def _split3(v):
    hi = v.astype(BF16)
    r1 = v - hi.astype(F32)
    mid = r1.astype(BF16)
    lo = (r1 - mid.astype(F32)).astype(BF16)
    return hi, mid, lo


def _dot_exact_right(v, m01):
    return sum(jnp.dot(p, m01, preferred_element_type=F32) for p in _split3(v))


def _dot_exact_left(m01, v):
    return sum(jnp.dot(m01, p, preferred_element_type=F32) for p in _split3(v))


def _mamba_kernel(z_ref, xbc_ref, xbcp_ref, dt_ref, dtt_ref, cw_ref, cb_ref, alr_ref, dbr_ref, alc_ref, dbc_ref,
                  dsk_ref, ng_ref, ex_ref, o_ref, hs, xcb, st_ref, ybuf,
                  *, q, taps, heads, hdim, groups, nstate):
    c = pl.program_id(1)
    d_inner = heads * hdim
    gn = groups * nstate
    gw = d_inner // groups
    hpg = heads // groups

    @pl.when(c == 0)
    def _():
        st_ref[...] = jnp.zeros(st_ref.shape, F32)

    hs[0, 0:HALO, :] = jnp.where(c != 0, xbcp_ref[...].astype(F32), 0.0)
    hs[0, HALO:HALO + q, :] = xbc_ref[...].astype(F32)
    _dwconv_rows(hs, cw_ref, cb_ref, xcb, tl=q, taps=taps, post=_silu)

    dt = _softplus(dt_ref[...] + dbr_ref[...])
    a = dt * (-jnp.exp(alr_ref[...]))
    dtt = _softplus(dtt_ref[...] + dbc_ref[...])
    at = dtt * (-jnp.exp(alc_ref[...]))
    ri = lax.broadcasted_iota(jnp.int32, (q, q), 0)
    ci = lax.broadcasted_iota(jnp.int32, (q, q), 1)
    lower = ri >= ci
    a_cum = _dot_exact_left(jnp.where(lower, 1.0, 0.0).astype(BF16), a)
    a_cum_t = _dot_exact_right(at, jnp.where(ri <= ci, 1.0, 0.0).astype(BF16))

    ex = ex_ref[...]
    dt_x = _dot_exact_right(dt, ex)
    acum_x = _dot_exact_right(a_cum, ex)
    alast_x = acum_x[q - 1:q, :]
    xs = xcb[:, 0:d_inner]
    xdt = xs * dt_x
    xdt_b = xdt.astype(BF16)
    xend_b = (xdt * jnp.exp(alast_x - acum_x)).astype(BF16)
    eac_x = jnp.exp(acum_x)
    cdec_x = jnp.exp(alast_x)

    for g in range(groups):
        bg = xcb[:, d_inner + g * nstate:d_inner + (g + 1) * nstate].astype(BF16)
        cg = xcb[:, d_inner + gn + g * nstate:d_inner + gn + (g + 1) * nstate].astype(BF16)
        cbm = lax.dot_general(cg, bg, (((1,), (1,)), ((), ())), preferred_element_type=F32)
        for r in range(hpg):
            h = g * hpg + r
            seg = a_cum[:, h:h + 1] - a_cum_t[h:h + 1, :]
            dec = jnp.exp(jnp.where(lower, seg, NEG_BIG))
            mm = (cbm * dec).astype(BF16)
            ybuf[:, h * hdim:(h + 1) * hdim] = jnp.dot(mm, xdt_b[:, h * hdim:(h + 1) * hdim],
                                                       preferred_element_type=F32)
        lanes = slice(g * gw, (g + 1) * gw)
        st = st_ref[g]
        ybuf[:, lanes] = ybuf[:, lanes] + jnp.dot(cg, st.astype(BF16), preferred_element_type=F32) * eac_x[:, lanes]
        new = lax.dot_general(bg, xend_b[:, lanes], (((0,), (0,)), ((), ())), preferred_element_type=F32)
        st_ref[g] = st * cdec_x[:, lanes] + new

    y = ybuf[...] + xs * dsk_ref[...]
    y = y * _silu(z_ref[...].astype(F32))
    y = y * lax.rsqrt(jnp.mean(y * y, axis=-1, keepdims=True) + EPS) * ng_ref[...]
    o_ref[...] = y.astype(o_ref.dtype)


def mamba2_mixer(u, dts, dtst, col_z, col_xbc, conv_w, conv_b, a_log, dt_bias, d_skip, norm_g,
                 *, bsz, seq, q, heads, hdim, groups, nstate):
    t = u.shape[0]
    d_inner = heads * hdim
    xw = d_inner + 2 * groups * nstate
    taps = conv_w.shape[0]
    nc = seq // q
    hb = q // HALO
    wpad = _tap_weights(conv_w)
    row = lambda v: jnp.zeros((1, LANES), F32).at[0, :heads].set(v)
    col = lambda v: v.reshape(heads, 1).astype(F32)
    expand = (jnp.arange(d_inner)[None, :] // hdim == jnp.arange(LANES)[:, None]).astype(BF16)
    kern = functools.partial(_mamba_kernel, q=q, taps=taps, heads=heads, hdim=hdim, groups=groups, nstate=nstate)
    rowi = lambda b, c: b * nc + c
    full = lambda shp: pl.BlockSpec(shp, lambda b, c: (0,) * len(shp))
    return pl.pallas_call(
        kern,
        grid=(bsz, nc),
        in_specs=[pl.BlockSpec((q, d_inner), lambda b, c: (rowi(b, c), col_z)),
                  pl.BlockSpec((q, xw), lambda b, c: (rowi(b, c), col_xbc)),
                  pl.BlockSpec((HALO, xw), lambda b, c: (jnp.maximum(rowi(b, c) * hb - 1, 0), col_xbc)),
                  pl.BlockSpec((q, LANES), lambda b, c: (rowi(b, c), 0)),
                  pl.BlockSpec((heads, q), lambda b, c: (0, rowi(b, c))),
                  full((taps, SUBLANES, xw)), full((1, xw)),
                  full((1, LANES)), full((1, LANES)), full((heads, 1)), full((heads, 1)),
                  full((1, d_inner)), full((1, d_inner)), full((LANES, d_inner))],
        out_specs=pl.BlockSpec((q, d_inner), lambda b, c: (rowi(b, c), 0)),
        out_shape=jax.ShapeDtypeStruct((t, d_inner), BF16),
        scratch_shapes=[pltpu.VMEM((len(_conv_shifts(taps)), HALO + q, xw), F32),
                        pltpu.VMEM((q, xw), F32),
                        pltpu.VMEM((groups, nstate, d_inner // groups), F32),
                        pltpu.VMEM((q, d_inner), F32)],
        compiler_params=_cparams("parallel", "arbitrary"),
        name="mamba2_ssd",
    )(u, u, u, dts, dtst, wpad, conv_b.reshape(1, xw), row(a_log), row(dt_bias), col(a_log), col(dt_bias),
      jnp.repeat(d_skip.astype(F32), hdim).reshape(1, d_inner), norm_g.reshape(1, d_inner), expand)


ROUTER_ROWS = 8


def _router_kernel(x_ref, rwt_ref, rb_ref, o_ref, *, groups, epg, top_k):
    lg = lax.dot_general(rwt_ref[...], x_ref[...], (((1,), (1,)), ((), ())), preferred_element_type=F32)
    sc = _sigmoid(lg)
    sel = sc + rb_ref[...]
    s = [sel[p * groups:(p + 1) * groups, :] for p in range(epg)]
    w = [sc[p * groups:(p + 1) * groups, :] for p in range(epg)]
    gs = None
    for p in range(epg):
        for r in range(p + 1, epg):
            pair = s[p] + s[r]
            gs = pair if gs is None else jnp.maximum(gs, pair)
    gidx = lax.broadcasted_iota(jnp.int32, gs.shape, 0)
    gmax = jnp.max(gs, axis=0, keepdims=True)
    gbest = jnp.min(jnp.where(gs == gmax, gidx, groups), axis=0, keepdims=True)
    pick = gidx == gbest
    v = [jnp.sum(jnp.where(pick, s[p], 0.0), axis=0, keepdims=True) for p in range(epg)]
    wv = [jnp.sum(jnp.where(pick, w[p], 0.0), axis=0, keepdims=True) for p in range(epg)]
    chosen = []
    for p in range(epg):
        rank = jnp.zeros_like(v[p])
        for r in range(epg):
            if r == p:
                continue
            ahead = (v[r] >= v[p]) if r < p else (v[r] > v[p])
            rank = rank + jnp.where(ahead, 1.0, 0.0)
        chosen.append(jnp.where(rank < float(top_k), wv[p], 0.0))
    tot = chosen[0]
    for p in range(1, epg):
        tot = tot + chosen[p]
    o_ref[...] = jnp.zeros(o_ref.shape, F32)
    for p in range(epg):
        o_ref[p:p + 1, :] = chosen[p] / tot
    o_ref[epg:epg + 1, :] = gbest.astype(F32)


def moe_router(xb, router_w, router_bias, *, groups, epg, top_k, tm):
    t, d = xb.shape
    e = groups * epg
    order = jnp.arange(e).reshape(groups, epg).T.reshape(-1)
    rwt = router_w.T[order].astype(BF16)
    rb = router_bias[order].reshape(e, 1).astype(F32)
    kern = functools.partial(_router_kernel, groups=groups, epg=epg, top_k=top_k)
    return pl.pallas_call(
        kern,
        grid=(t // tm,),
        in_specs=[pl.BlockSpec((tm, d), lambda i: (i, 0)),
                  pl.BlockSpec((e, d), lambda i: (0, 0)),
                  pl.BlockSpec((e, 1), lambda i: (0, 0))],
        out_specs=pl.BlockSpec((ROUTER_ROWS, tm), lambda i: (0, i)),
        out_shape=jax.ShapeDtypeStruct((ROUTER_ROWS, t), F32),
        compiler_params=_cparams("parallel"),
        name="moe_router",
    )(xb, rwt, rb)


PREP_ROWS = 1024


def _moe_prep_up_kernel(g_ref, u_ref, go_ref, uo_ref):
    epg, _, f = g_ref.shape
    for p in range(epg):
        go_ref[:, p * f:(p + 1) * f] = g_ref[p].astype(BF16)
        uo_ref[:, p * f:(p + 1) * f] = u_ref[p].astype(BF16)


def _moe_prep_down_kernel(d_ref, do_ref):
    do_ref[...] = d_ref[...].astype(BF16)


def moe_weight_prep(w_gate, w_up, w_down, layer, groups, epg):
    _, e, d, f = w_gate.shape
    rows = min(PREP_ROWS, d)
    up_in = pl.BlockSpec((None, epg, rows, f), lambda g, c: (layer, g, c, 0))
    up_out = pl.BlockSpec((None, rows, epg * f), lambda g, c: (g, c, 0))
    wg, wu = pl.pallas_call(
        _moe_prep_up_kernel,
        grid=(groups, d // rows),
        in_specs=[up_in, up_in],
        out_specs=[up_out, up_out],
        out_shape=[jax.ShapeDtypeStruct((groups, d, epg * f), BF16)] * 2,
        compiler_params=_cparams("parallel", "parallel"),
        name="moe_weight_prep_up",
    )(w_gate, w_up)
    wd = pl.pallas_call(
        _moe_prep_down_kernel,
        grid=(e,),
        in_specs=[pl.BlockSpec((None, None, f, d), lambda ei: (layer, ei, 0, 0))],
        out_specs=pl.BlockSpec((None, f, d), lambda ei: (ei // epg, ei % epg, 0)),
        out_shape=jax.ShapeDtypeStruct((groups, epg * f, d), BF16),
        compiler_params=_cparams("parallel"),
        name="moe_weight_prep_down",
    )(w_down)
    return wg, wu, wd


def _moe_up_kernel(tg_ref, x_ref, gate_ref, wg_ref, wu_ref, h_ref, *, epg, dff):
    x = x_ref[...]
    g = jnp.dot(x, wg_ref[...], preferred_element_type=F32)
    u = jnp.dot(x, wu_ref[...], preferred_element_type=F32)
    h = _silu(g) * u
    gates = gate_ref[...]
    for p in range(epg):
        h_ref[:, p * dff:(p + 1) * dff] = (h[:, p * dff:(p + 1) * dff] * gates[:, p:p + 1]).astype(h_ref.dtype)


def _moe_down_kernel(tg_ref, h_ref, wd_ref, x_ref, g_ref, b_ref, o_ref, *, alpha):
    ffn = jnp.dot(h_ref[...], wd_ref[...], preferred_element_type=F32)
    o_ref[...] = _layer_norm_rows(alpha * x_ref[...] + ffn, g_ref[...], b_ref[...])


def moe_experts(xs_b, xs_f, gates_s, tile_group, wg, wu, wd, ln_g, ln_b, *, alpha, epg, tm):
    p_rows, d = xs_b.shape
    f4 = wg.shape[2]
    dff = f4 // epg
    n_tiles = p_rows // tm
    once = pl.Buffered(1)
    h = pl.pallas_call(
        functools.partial(_moe_up_kernel, epg=epg, dff=dff),
        grid_spec=pltpu.PrefetchScalarGridSpec(
            num_scalar_prefetch=1,
            grid=(n_tiles,),
            in_specs=[pl.BlockSpec((tm, d), lambda i, tg: (i, 0)),
                      pl.BlockSpec((tm, epg), lambda i, tg: (i, 0)),
                      pl.BlockSpec((None, d, f4), lambda i, tg: (tg[i], 0, 0), pipeline_mode=once),
                      pl.BlockSpec((None, d, f4), lambda i, tg: (tg[i], 0, 0), pipeline_mode=once)],
            out_specs=pl.BlockSpec((tm, f4), lambda i, tg: (i, 0))),
        out_shape=jax.ShapeDtypeStruct((p_rows, f4), BF16),
        compiler_params=_cparams("arbitrary"),
        name="moe_gate_up",
    )(tile_group, xs_b, gates_s, wg, wu)
    return pl.pallas_call(
        functools.partial(_moe_down_kernel, alpha=alpha),
        grid_spec=pltpu.PrefetchScalarGridSpec(
            num_scalar_prefetch=1,
            grid=(n_tiles,),
            in_specs=[pl.BlockSpec((tm, f4), lambda i, tg: (i, 0)),
                      pl.BlockSpec((None, f4, d), lambda i, tg: (tg[i], 0, 0), pipeline_mode=once),
                      pl.BlockSpec((tm, d), lambda i, tg: (i, 0)),
                      pl.BlockSpec((1, d), lambda i, tg: (0, 0)),
                      pl.BlockSpec((1, d), lambda i, tg: (0, 0))],
            out_specs=pl.BlockSpec((tm, d), lambda i, tg: (i, 0))),
        out_shape=jax.ShapeDtypeStruct((p_rows, d), F32),
        compiler_params=_cparams("arbitrary"),
        name="moe_down_ln",
    )(tile_group, h, wd, xs_f, ln_g, ln_b)


def moe_block(x, xb, router_w, router_bias, wg, wu, wd, ln_g, ln_b, *, alpha, groups, epg, top_k, tm):
    t, d = x.shape
    rt = moe_router(xb, router_w, router_bias, groups=groups, epg=epg, top_k=top_k, tm=512)
    gbest = rt[epg].astype(jnp.int32)
    gates = rt[:epg].T
    onehot = (gbest[:, None] == jnp.arange(groups)[None, :]).astype(jnp.int32)
    counts = jnp.sum(onehot, axis=0)
    padded = ((counts + tm - 1) // tm) * tm
    ends = jnp.cumsum(padded)
    rank = jnp.take_along_axis(jnp.cumsum(onehot, axis=0) - onehot, gbest[:, None], axis=1)[:, 0]
    pos = (ends - padded)[gbest] + rank
    p_rows = t + groups * tm
    src = jnp.zeros((p_rows,), jnp.int32).at[pos].set(jnp.arange(t, dtype=jnp.int32))
    gates_s = jnp.zeros((p_rows, epg), F32).at[pos].set(gates)
    tile_start = jnp.arange(p_rows // tm, dtype=jnp.int32) * tm
    tile_group = jnp.minimum(jnp.sum(tile_start[:, None] >= ends[None, :], axis=1), groups - 1).astype(jnp.int32)
    xs_b = xb[src]
    xs_f = x[src]
    out_s = moe_experts(xs_b, xs_f, gates_s, tile_group, wg, wu, wd, ln_g, ln_b, alpha=alpha, epg=epg, tm=tm)
    x2 = out_s[pos]
    return x2, x2.astype(BF16)


def kernel(x, even_w_in, even_conv_a_w, even_conv_a_b, even_ln_a_g, even_ln_a_b, even_kv_norm_g, even_w_uk, even_w_uv, even_w_out, odd_w_in, odd_conv_c_w, odd_conv_d_w, odd_conv_d_b, odd_a_log, odd_dt_bias, odd_d_skip, odd_norm_g, odd_w_out, ln1_g, ln1_b, ln2_g, ln2_b, router_w, router_bias, moe_w_gate, moe_w_up, moe_w_down):
    bsz, seq, d = x.shape
    t = bsz * seq
    depth = ln1_g.shape[0]
    alpha = (2 * depth) ** 0.25

    a_ch = even_conv_a_w.shape[2]
    heads, qk_dim, rank = even_w_uk.shape[1:]
    v_dim = even_w_uv.shape[3]
    idx_dim, idx_heads = 64, 16
    q_w = heads * qk_dim
    qi_w = idx_heads * idx_dim
    topk = min(256, seq // 4)

    c_ch = odd_conv_c_w.shape[2]
    d_heads = odd_a_log.shape[1]
    d_inner = odd_norm_g.shape[1]
    xw = odd_conv_d_w.shape[2]
    d_groups, d_state, d_chunk = 8, 128, 128

    n_experts, _, dff = moe_w_gate.shape[1:]
    groups, top_k = 8, 2
    epg = n_experts // groups

    xf = x.reshape(t, d)
    xb = xf.astype(BF16)

    def pad_rows(w):
        return jnp.pad(w, ((0, LANES - w.shape[0]), (0, 0)))

    even_wt = jnp.swapaxes(even_w_in, 1, 2)
    odd_wt = jnp.swapaxes(odd_w_in, 1, 2)
    proj_tm, proj_tn = 512, 1536
    for layer in range(depth):
        i = layer // 2
        if layer % 2 == 0:
            o_q = 2 * a_ch
            o_ckv = o_q + q_w
            o_qi = o_ckv + rank
            o_k = o_qi + qi_w
            u = matmul(xb, even_wt, i, o_k, BF16, proj_tm, proj_tn)
            usm, _ = matmul_small(xb, pad_rows(even_wt[i, o_k:, :]), 512)
            y_a, ckvn = conformer_conv(u, 0, 1, o_ckv // rank, a_ch, rank,
                                       even_conv_a_w[i], even_conv_a_b[i], even_ln_a_g[i], even_ln_a_b[i],
                                       even_kv_norm_g[i], seq, 256)
            y_b = dsa_attention(u.reshape(bsz, seq, -1), usm.reshape(bsz, seq, LANES), ckvn.reshape(bsz, seq, rank),
                                even_w_uk[i].astype(BF16), even_w_uv[i].astype(BF16),
                                col_q=o_q // q_w, col_qi=o_qi // (qi_w // 2), idx_dim=idx_dim, idx_heads=idx_heads,
                                topk=topk, sb_rows=512, tk=512, hg=4).reshape(t, heads * v_dim)
            w_out = even_w_out[i].astype(BF16)
            y1, y2 = y_a, y_b
        else:
            o_dt = 3 * c_ch + d_inner + xw
            u = matmul(xb, odd_wt, i, o_dt, BF16, proj_tm, proj_tn)
            dts, dtst = matmul_small(xb, pad_rows(odd_wt[i, o_dt:, :]), 512)
            y_c = short_gated_conv(u, 0, 1, 2, c_ch, odd_conv_c_w[i], seq, 256)
            y_d = mamba2_mixer(u, dts, dtst, 3 * c_ch // d_inner, (3 * c_ch + d_inner) // xw,
                               odd_conv_d_w[i], odd_conv_d_b[i], odd_a_log[i], odd_dt_bias[i], odd_d_skip[i],
                               odd_norm_g[i], bsz=bsz, seq=seq, q=d_chunk, heads=d_heads, hdim=d_inner // d_heads,
                               groups=d_groups, nstate=d_state)
            w_out = odd_w_out[i].astype(BF16)
            y1, y2 = y_c, y_d
        xf, xb = matmul_residual_ln(y1, y2, w_out, xf, ln1_g[layer].reshape(1, d), ln1_b[layer].reshape(1, d),
                                    alpha, 1024, 512)
        wg, wu, wd = moe_weight_prep(moe_w_gate, moe_w_up, moe_w_down, layer, groups, epg)
        xf, xb = moe_block(xf, xb, router_w, router_bias, wg, wu, wd,
                           ln2_g[layer].reshape(1, d), ln2_b[layer].reshape(1, d),
                           alpha=alpha, groups=groups, epg=epg, top_k=top_k, tm=256)
    return xf.reshape(bsz, seq, d)
```
